```python
import math
import jax, jax.numpy as jnp
from jax import lax
import numpy as np

D_MODEL = 1024
BATCH = 8
SEQ = 4096
DEPTH = 2

N_A = DEPTH // 2
N_B = DEPTH - N_A

D_FF = 2816

D_RNN = 1280
N_GATE_BLOCKS = 5
GATE_BW = D_RNN // N_GATE_BLOCKS
CONV_WIDTH = 4
LRU_C = 8.0

N_HEADS = 16
HEAD_DIM = D_MODEL // N_HEADS
DILATION_PAIRS = ((128, 1), (512, 4), (2048, 16))
N_GROUPS = len(DILATION_PAIRS)

DEEPNORM_ALPHA = float((2 * DEPTH) ** 0.25)
DEEPNORM_BETA = float((8 * DEPTH) ** -0.25)

LN_EPS = 1e-5
NEG_INF = -1e30

kernel_name = "yoco_rglru_dilated_alibi_macaron_deepnorm"


def layer_norm(x, g, b):
    xf = x.astype(jnp.float32)
    mu = jnp.mean(xf, axis=-1, keepdims=True)
    xc = xf - mu
    var = jnp.mean(xc * xc, axis=-1, keepdims=True)
    y = xc * lax.rsqrt(var + LN_EPS) * g.astype(jnp.float32) + b.astype(jnp.float32)
    return y.astype(x.dtype)


def swiglu(x, w_in, w_out):
    gate, up = jnp.split(x @ w_in, 2, axis=-1)
    return (jax.nn.silu(gate) * up) @ w_out


def causal_depthwise_conv(u, w, b):
    r = u.shape[-1]
    rhs = w.astype(u.dtype)[:, None, :]
    y = lax.conv_general_dilated(
        u, rhs, window_strides=(1,), padding=[(CONV_WIDTH - 1, 0)],
        dimension_numbers=("NWC", "WIO", "NWC"), feature_group_count=r)
    return y + b.astype(u.dtype)


def rg_lru(u, gate_w, gate_b, lam):
    uf = u.astype(jnp.float32)
    bsz, s, r = uf.shape
    ublk = uf.reshape(bsz, s, N_GATE_BLOCKS, GATE_BW)
    gates = jnp.einsum("bsnc,gncd->gbsnd", ublk, gate_w.astype(jnp.float32)).reshape(2, bsz, s, r)
    gates = gates + gate_b.astype(jnp.float32)[:, None, None, :]
    rec_gate = jax.nn.sigmoid(gates[0])
    in_gate = jax.nn.sigmoid(gates[1])
    log_a = -LRU_C * rec_gate * jax.nn.softplus(-lam.astype(jnp.float32))
    a = jnp.exp(log_a)
    b = jnp.sqrt(-jnp.expm1(2.0 * log_a)) * (in_gate * uf)

    def combine(left, right):
        a1, b1 = left
        a2, b2 = right
        return a1 * a2, a2 * b1 + b2

    _, h = lax.associative_scan(combine, (a, b), axis=1)
    return h


def rglru_block(x, w_in, conv_w, conv_b, gate_w, gate_b, lam, w_out):
    y_br, u_br = jnp.split(x @ w_in, 2, axis=-1)
    y = jax.nn.gelu(y_br)
    u = causal_depthwise_conv(u_br, conv_w, conv_b)
    h = rg_lru(u, gate_w, gate_b, lam)
    return (y * h.astype(y.dtype)) @ w_out


def alibi_slopes(n):
    return jnp.exp2(-8.0 * (jnp.arange(n, dtype=jnp.float32) + 1.0) / n)


def dilated_group(q, k, v, slopes, window, dilation):
    bsz, s, h, dh = q.shape
    reach = window // dilation
    blk = reach
    unit = dilation * blk
    s_pad = -(-s // unit) * unit
    length = s_pad // dilation
    nb = length // blk

    def to_blocks(t):
        t = jnp.pad(t, ((0, 0), (0, s_pad - s), (0, 0), (0, 0)))
        t = t.reshape(bsz, length, dilation, h, dh).transpose(0, 2, 1, 3, 4)
        return t.reshape(bsz, dilation, nb, blk, h, dh)

    def with_prev(t):
        prev = jnp.pad(t, ((0, 0), (0, 0), (1, 0), (0, 0), (0, 0), (0, 0)))[:, :, :-1]
        return jnp.concatenate([prev, t], axis=3)

    qb = to_blocks(q)
    kw = with_prev(to_blocks(k))
    vw = with_prev(to_blocks(v))

    scores = jnp.einsum("bpnqhd,bpnkhd->bpnhqk", qb, kw) * (1.0 / math.sqrt(dh))
    q_idx = jnp.arange(blk)[:, None] + blk
    k_idx = jnp.arange(2 * blk)[None, :]
    dist = q_idx - k_idx
    key_abs = jnp.arange(nb)[:, None] * blk - blk + k_idx
    valid = ((dist >= 0) & (dist <= reach))[None] & (key_abs >= 0)[:, None, :]
    bias = -slopes[:, None, None] * (dist * dilation).astype(jnp.float32)[None]
    scores = jnp.where(valid[:, None], scores + bias, NEG_INF)

    m = jnp.max(scores, axis=-1, keepdims=True)
    p = jnp.exp(scores - m)
    denom = jnp.sum(p, axis=-1, keepdims=True)
    lse = (m + jnp.log(denom))[..., 0]
    out = jnp.einsum("bpnhqk,bpnkhd->bpnqhd", p / denom, vw)

    out = out.reshape(bsz, dilation, length, h, dh).transpose(0, 2, 1, 3, 4).reshape(bsz, s_pad, h, dh)[:, :s]
    lse = lse.transpose(0, 1, 2, 4, 3).reshape(bsz, dilation, length, h).transpose(0, 2, 1, 3).reshape(bsz, s_pad, h)[:, :s]
    return out, lse


def dilated_attention(x, k, v, w_q, w_o):
    bsz, s, _ = x.shape
    q = (x @ w_q).reshape(bsz, s, N_GROUPS, N_HEADS, HEAD_DIM).astype(jnp.float32)
    slopes = alibi_slopes(N_HEADS)
    outs, lses = [], []
    for g, (window, dilation) in enumerate(DILATION_PAIRS):
        o, l = dilated_group(q[:, :, g], k, v, slopes, window, dilation)
        outs.append(o)
        lses.append(l)
    wts = jax.nn.softmax(jnp.stack(lses, axis=0), axis=0)
    out = jnp.sum(wts[..., None] * jnp.stack(outs, axis=0), axis=0)
    return out.reshape(bsz, s, N_HEADS * HEAD_DIM).astype(x.dtype) @ w_o


def setup_inputs(seed: int = 0) -> dict:
    key = jax.random.key(seed)
    ks = jax.random.split(key, 20)
    f32 = jnp.float32
    nrm = lambda k, shape, scale: jax.random.normal(k, shape, f32) * scale

    x = jax.random.normal(ks[0], (BATCH, SEQ, D_MODEL), f32)
    ln_g = 1.0 + nrm(ks[1], (DEPTH, 3, D_MODEL), 0.02)
    ln_b = nrm(ks[2], (DEPTH, 3, D_MODEL), 0.02)
    ffn_w_in = nrm(ks[3], (DEPTH, 2, D_MODEL, 2 * D_FF), D_MODEL ** -0.5)
    ffn_w_out = nrm(ks[4], (DEPTH, 2, D_FF, D_MODEL), D_FF ** -0.5 * DEEPNORM_BETA)

    rg_w_in = nrm(ks[5], (N_A, D_MODEL, 2 * D_RNN), D_MODEL ** -0.5)
    rg_conv_w = nrm(ks[6], (N_A, CONV_WIDTH, D_RNN), CONV_WIDTH ** -0.5)
    rg_conv_b = nrm(ks[7], (N_A, D_RNN), 0.02)
    rg_gate_w = nrm(ks[8], (N_A, 2, N_GATE_BLOCKS, GATE_BW, GATE_BW), GATE_BW ** -0.5)
    rg_gate_b = nrm(ks[9], (N_A, 2, D_RNN), 0.02)
    a0 = jax.random.uniform(ks[10], (N_A, D_RNN), f32, 0.9, 0.999)
    rg_lam = jnp.log(a0) - jnp.log1p(-a0)
    rg_w_out = nrm(ks[11], (N_A, D_RNN, D_MODEL), D_RNN ** -0.5 * DEEPNORM_BETA)

    kv_w = nrm(ks[12], (D_MODEL, 2 * N_HEADS * HEAD_DIM), D_MODEL ** -0.5)
    attn_w_q = nrm(ks[13], (N_B, D_MODEL, N_GROUPS * N_HEADS * HEAD_DIM), D_MODEL ** -0.5)
    attn_w_o = nrm(ks[14], (N_B, N_HEADS * HEAD_DIM, D_MODEL), (N_HEADS * HEAD_DIM) ** -0.5 * DEEPNORM_BETA)
    return {
        "x": x, "ln_g": ln_g, "ln_b": ln_b, "ffn_w_in": ffn_w_in, "ffn_w_out": ffn_w_out,
        "rg_w_in": rg_w_in, "rg_conv_w": rg_conv_w, "rg_conv_b": rg_conv_b,
        "rg_gate_w": rg_gate_w, "rg_gate_b": rg_gate_b, "rg_lam": rg_lam, "rg_w_out": rg_w_out,
        "kv_w": kv_w, "attn_w_q": attn_w_q, "attn_w_o": attn_w_o,
    }


def reference(x, ln_g, ln_b, ffn_w_in, ffn_w_out, rg_w_in, rg_conv_w, rg_conv_b,
              rg_gate_w, rg_gate_b, rg_lam, rg_w_out, kv_w, attn_w_q, attn_w_o):
    bsz, s, _ = x.shape
    k_shared = None
    v_shared = None
    for layer in range(DEPTH):
        x = layer_norm(DEEPNORM_ALPHA * x + 0.5 * swiglu(x, ffn_w_in[layer, 0], ffn_w_out[layer, 0]),
                       ln_g[layer, 0], ln_b[layer, 0])
        if layer < N_A:
            j = layer
            mix = rglru_block(x, rg_w_in[j], rg_conv_w[j], rg_conv_b[j], rg_gate_w[j],
                              rg_gate_b[j], rg_lam[j], rg_w_out[j])
        else:
            j = layer - N_A
            mix = dilated_attention(x, k_shared, v_shared, attn_w_q[j], attn_w_o[j])
        x = layer_norm(DEEPNORM_ALPHA * x + mix, ln_g[layer, 1], ln_b[layer, 1])
        x = layer_norm(DEEPNORM_ALPHA * x + 0.5 * swiglu(x, ffn_w_in[layer, 1], ffn_w_out[layer, 1]),
                       ln_g[layer, 2], ln_b[layer, 2])
        if layer == N_A - 1:
            kv = (x @ kv_w).reshape(bsz, s, 2, N_HEADS, HEAD_DIM).astype(jnp.float32)
            k_shared = kv[:, :, 0]
            v_shared = kv[:, :, 1]
    return x
```

```python
import functools
import math

import jax
import jax.numpy as jnp
from jax import lax
from jax.experimental import pallas as pl
from jax.experimental.pallas import tpu as pltpu

D_MODEL = 1024
BATCH = 8
SEQ = 4096
DEPTH = 2
N_A = DEPTH // 2
D_FF = 2816
D_RNN = 1280
N_GATE_BLOCKS = 5
GATE_BW = D_RNN // N_GATE_BLOCKS
CONV_WIDTH = 4
LRU_C = 8.0
N_HEADS = 16
HEAD_DIM = D_MODEL // N_HEADS
DILATION_PAIRS = ((128, 1), (512, 4), (2048, 16))
N_GROUPS = len(DILATION_PAIRS)
DEEPNORM_ALPHA = float((2 * DEPTH) ** 0.25)
LN_EPS = 1e-5
NEG_INF = -1e30

TOKENS = BATCH * SEQ
ROW_TILE = 512
TIME_TILE = ROW_TILE // BATCH
FF_CHUNK = 256
ATT_BLK = 128
ATT_QROWS = 256
LSE_LANES = 128
VMEM_LIMIT = 56 * 1024 * 1024

BF16 = jnp.bfloat16
F32 = jnp.float32


def _dot(a, b):
    return jnp.dot(a, b, preferred_element_type=F32)


def _dot_nt(a, b):
    return lax.dot_general(a, b, (((1,), (1,)), ((), ())), preferred_element_type=F32)


def _sigmoid(x):
    return 0.5 * jnp.tanh(0.5 * x) + 0.5


def _gelu_tanh(x):
    c = math.sqrt(2.0 / math.pi)
    return 0.5 * x * (1.0 + jnp.tanh(c * (x + 0.044715 * (x * x * x))))


def _layer_norm(z, g, b):
    mu = jnp.mean(z, axis=-1, keepdims=True)
    zc = z - mu
    var = jnp.mean(zc * zc, axis=-1, keepdims=True)
    return zc * lax.rsqrt(var + LN_EPS) * g + b


def _resident(shape):
    nd = len(shape)
    return pl.BlockSpec(shape, lambda *_: (0,) * nd, pipeline_mode=pl.Buffered(1))


def _params(n_grid):
    return pltpu.CompilerParams(dimension_semantics=("arbitrary",) * n_grid,
                                vmem_limit_bytes=VMEM_LIMIT)


def _ffn_kernel(x_ref, w_in_ref, w_out_ref, g_ref, b_ref, o_ref, act_ref):
    x = x_ref[...]
    xb = x.astype(BF16)
    for c in range(D_FF // FF_CHUNK):
        lo = c * FF_CHUNK
        gate = _dot(xb, w_in_ref[:, lo:lo + FF_CHUNK])
        up = _dot(xb, w_in_ref[:, D_FF + lo:D_FF + lo + FF_CHUNK])
        act_ref[:, lo:lo + FF_CHUNK] = (gate * _sigmoid(gate) * up).astype(BF16)
    y = _dot(act_ref[...], w_out_ref[...])
    z = DEEPNORM_ALPHA * x + 0.5 * y
    o_ref[...] = _layer_norm(z, g_ref[...], b_ref[...])


def _ffn_ln(x, w_in, w_out, g, b, *, in_bsd=False, out_bsd=False):
    n_t = SEQ // ROW_TILE
    if in_bsd or out_bsd:
        grid = (n_t, BATCH)
        tm = ROW_TILE
    else:
        grid = (TOKENS // ROW_TILE,)
        tm = ROW_TILE
    if in_bsd:
        x_spec = pl.BlockSpec((None, tm, D_MODEL), lambda i, bb: (bb, i, 0))
    elif out_bsd:
        x = x.reshape(SEQ, BATCH * D_MODEL)
        x_spec = pl.BlockSpec((tm, D_MODEL), lambda i, bb: (i, bb))
    else:
        x_spec = pl.BlockSpec((tm, D_MODEL), lambda i: (i, 0))
    if out_bsd:
        o_shape = jax.ShapeDtypeStruct((BATCH, SEQ, D_MODEL), F32)
        o_spec = pl.BlockSpec((None, tm, D_MODEL), lambda i, bb: (bb, i, 0))
    elif in_bsd:
        o_shape = jax.ShapeDtypeStruct((SEQ, BATCH * D_MODEL), F32)
        o_spec = pl.BlockSpec((tm, D_MODEL), lambda i, bb: (i, bb))
    else:
        o_shape = jax.ShapeDtypeStruct((TOKENS, D_MODEL), F32)
        o_spec = pl.BlockSpec((tm, D_MODEL), lambda i: (i, 0))
    out = pl.pallas_call(
        _ffn_kernel,
        grid=grid,
        in_specs=[x_spec, _resident((D_MODEL, 2 * D_FF)), _resident((D_FF, D_MODEL)),
                  _resident((1, D_MODEL)), _resident((1, D_MODEL))],
        out_specs=o_spec,
        out_shape=o_shape,
        scratch_shapes=[pltpu.VMEM((tm, D_FF), BF16)],
        compiler_params=_params(len(grid)),
        name="ffn_ln",
    )(x, w_in, w_out, g, b)
    if in_bsd and not out_bsd:
        out = out.reshape(TOKENS, D_MODEL)
    return out


def _rglru_kernel(x_ref, w_in_ref, conv_w_ref, conv_b_ref, gate_w_ref, gate_b_ref, lam_ref,
                  w_out_ref, g_ref, b_ref, o_ref, ubuf_ref, a_ref, h_ref, carry_ref):
    tm = ROW_TILE
    halo = (CONV_WIDTH - 1) * BATCH

    @pl.when(pl.program_id(0) == 0)
    def _():
        ubuf_ref[0:halo, :] = jnp.zeros((halo, D_RNN), F32)
        carry_ref[...] = jnp.zeros((BATCH, D_RNN), F32)

    x = x_ref[...]
    xb = x.astype(BF16)
    y_br = _dot(xb, w_in_ref[:, 0:D_RNN])
    u_br = _dot(xb, w_in_ref[:, D_RNN:2 * D_RNN])

    ubuf_ref[halo:halo + tm, :] = u_br
    u = conv_b_ref[...] + conv_w_ref[CONV_WIDTH - 1:CONV_WIDTH, :] * u_br
    for j in range(CONV_WIDTH - 1):
        u = u + conv_w_ref[j:j + 1, :] * ubuf_ref[j * BATCH:j * BATCH + tm, :]
    ubuf_ref[0:halo, :] = u_br[tm - halo:tm, :]

    ub = u.astype(BF16)
    rec_parts, in_parts = [], []
    for n in range(N_GATE_BLOCKS):
        blk = ub[:, n * GATE_BW:(n + 1) * GATE_BW]
        rec_parts.append(_dot(blk, gate_w_ref[0, n]))
        in_parts.append(_dot(blk, gate_w_ref[1, n]))
    rec_gate = _sigmoid(jnp.concatenate(rec_parts, axis=1) + gate_b_ref[0:1, :])
    in_gate = _sigmoid(jnp.concatenate(in_parts, axis=1) + gate_b_ref[1:2, :])

    neg_lam = -lam_ref[...]
    softplus = jnp.maximum(neg_lam, 0.0) + jnp.log1p(jnp.exp(-jnp.abs(neg_lam)))
    log_a = (-LRU_C) * rec_gate * softplus
    a = jnp.exp(log_a)
    a_ref[...] = a
    h_ref[...] = jnp.sqrt(1.0 - a * a) * (in_gate * u)

    def step(t, h):
        r = pl.multiple_of(t * BATCH, BATCH)
        h = a_ref[pl.ds(r, BATCH), :] * h + h_ref[pl.ds(r, BATCH), :]
        h_ref[pl.ds(r, BATCH), :] = h
        return h

    carry_ref[...] = lax.fori_loop(0, TIME_TILE, step, carry_ref[...], unroll=8)

    mixed = (_gelu_tanh(y_br) * h_ref[...]).astype(BF16)
    z = DEEPNORM_ALPHA * x + _dot(mixed, w_out_ref[...])
    o_ref[...] = _layer_norm(z, g_ref[...], b_ref[...])


def _rglru_ln(x, w_in, conv_w, conv_b, gate_w, gate_b, lam, w_out, g, b):
    tm = ROW_TILE
    halo = (CONV_WIDTH - 1) * BATCH
    return pl.pallas_call(
        _rglru_kernel,
        grid=(TOKENS // tm,),
        in_specs=[pl.BlockSpec((tm, D_MODEL), lambda i: (i, 0)),
                  _resident((D_MODEL, 2 * D_RNN)), _resident((CONV_WIDTH, D_RNN)),
                  _resident((1, D_RNN)), _resident((2, N_GATE_BLOCKS, GATE_BW, GATE_BW)),
                  _resident((2, D_RNN)), _resident((1, D_RNN)), _resident((D_RNN, D_MODEL)),
                  _resident((1, D_MODEL)), _resident((1, D_MODEL))],
        out_specs=pl.BlockSpec((tm, D_MODEL), lambda i: (i, 0)),
        out_shape=jax.ShapeDtypeStruct((TOKENS, D_MODEL), F32),
        scratch_shapes=[pltpu.VMEM((tm + halo, D_RNN), F32), pltpu.VMEM((tm, D_RNN), F32),
                        pltpu.VMEM((tm, D_RNN), F32), pltpu.VMEM((BATCH, D_RNN), F32)],
        compiler_params=_params(1),
        name="rglru_ln",
    )(x, w_in, conv_w, conv_b, gate_w, gate_b, lam, w_out, g, b)


def _proj_kernel(x_ref, w_ref, *o_refs, scale):
    xb = x_ref[...].astype(BF16)
    for n, o_ref in enumerate(o_refs):
        y = _dot(xb, w_ref[:, n * D_MODEL:(n + 1) * D_MODEL])
        if scale != 1.0:
            y = y * scale
        o_ref[...] = y.astype(BF16)


def _proj(x, w, n_out, scale=1.0):
    tm = ROW_TILE
    return pl.pallas_call(
        functools.partial(_proj_kernel, scale=scale),
        grid=(TOKENS // tm,),
        in_specs=[pl.BlockSpec((tm, D_MODEL), lambda i: (i, 0)),
                  _resident((D_MODEL, n_out * D_MODEL))],
        out_specs=[pl.BlockSpec((tm, D_MODEL), lambda i: (i, 0))] * n_out,
        out_shape=[jax.ShapeDtypeStruct((TOKENS, D_MODEL), BF16)] * n_out,
        compiler_params=_params(1),
        name=f"proj{n_out}",
    )(x, w)


def _attn_kernel(q_ref, kp_ref, kc_ref, vp_ref, vc_ref, o_ref, lse_ref, *, dilation):
    first = pl.program_id(2) == 0
    row = lax.broadcasted_iota(jnp.int32, (ATT_BLK, 2 * ATT_BLK), 0)
    col = lax.broadcasted_iota(jnp.int32, (ATT_BLK, 2 * ATT_BLK), 1)
    dist_i = row + ATT_BLK - col
    valid = (dist_i >= 0) & (dist_i <= ATT_BLK)
    dist = dist_i.astype(F32)
    lane_lo = lax.broadcasted_iota(jnp.int32, (1, 2 * HEAD_DIM), 1) < HEAD_DIM
    lse_ref[...] = jnp.zeros(lse_ref.shape, F32)

    for j in range(ATT_QROWS // ATT_BLK):
        r0 = j * ATT_BLK
        if j == 0:
            valid_j = valid & (col >= jnp.where(first, ATT_BLK, 0))
        else:
            valid_j = valid
        for pair in range(N_HEADS // 2):
            lanes = slice(pair * 2 * HEAD_DIM, (pair + 1) * 2 * HEAD_DIM)
            q2 = q_ref[r0:r0 + ATT_BLK, lanes]
            if j == 0:
                kw = jnp.concatenate([kp_ref[:, lanes], kc_ref[0:ATT_BLK, lanes]], axis=0)
                vw = jnp.concatenate([vp_ref[:, lanes], vc_ref[0:ATT_BLK, lanes]], axis=0)
            else:
                kw = kc_ref[r0 - ATT_BLK:r0 + ATT_BLK, lanes]
                vw = vc_ref[r0 - ATT_BLK:r0 + ATT_BLK, lanes]
            probs, v_parts, inv_l = [], [], []
            for e in range(2):
                head = 2 * pair + e
                slope = 2.0 ** (-8.0 * (head + 1) / N_HEADS)
                sel = lane_lo if e == 0 else jnp.logical_not(lane_lo)
                zero = jnp.zeros((), BF16)
                s = _dot_nt(q2, jnp.where(sel, kw, zero))
                s = jnp.where(valid_j, s + dist * (-slope * dilation), NEG_INF)
                m = jnp.max(s, axis=-1, keepdims=True)
                p = jnp.exp(s - m)
                l = jnp.sum(p, axis=-1, keepdims=True)
                probs.append(p.astype(BF16))
                v_parts.append(jnp.where(sel, vw, zero))
                inv_l.append(1.0 / l)
                lse_ref[r0:r0 + ATT_BLK, head:head + 1] = m + jnp.log(l)
            acc = _dot(jnp.concatenate(probs, axis=1), jnp.concatenate(v_parts, axis=0))
            o_ref[r0:r0 + ATT_BLK, lanes] = (acc * jnp.where(lane_lo, inv_l[0], inv_l[1])).astype(BF16)


def _attention_group(q, k, v, dilation):
    s_d = SEQ // dilation
    n_cb = dilation * BATCH
    qv = q.reshape(s_d, n_cb * D_MODEL)
    kv_ = k.reshape(s_d, n_cb * D_MODEL)
    vv = v.reshape(s_d, n_cb * D_MODEL)
    ratio = ATT_QROWS // ATT_BLK
    cur = pl.BlockSpec((ATT_QROWS, D_MODEL), lambda p, bb, t: (t, p * BATCH + bb))
    prev = pl.BlockSpec((ATT_BLK, D_MODEL),
                        lambda p, bb, t: (jnp.maximum(t * ratio - 1, 0), p * BATCH + bb))
    out, lse = pl.pallas_call(
        functools.partial(_attn_kernel, dilation=dilation),
        grid=(dilation, BATCH, s_d // ATT_QROWS),
        in_specs=[cur, prev, cur, prev, cur],
        out_specs=[cur, pl.BlockSpec((ATT_QROWS, LSE_LANES), lambda p, bb, t: (t, p * BATCH + bb))],
        out_shape=[jax.ShapeDtypeStruct((s_d, n_cb * D_MODEL), BF16),
                   jax.ShapeDtypeStruct((s_d, n_cb * LSE_LANES), F32)],
        compiler_params=_params(3),
        name=f"attn_d{dilation}",
    )(qv, kv_, kv_, vv, vv)
    return out.reshape(TOKENS, D_MODEL), lse.reshape(TOKENS, LSE_LANES)


def _attn_out_kernel(x_ref, o0_ref, o1_ref, o2_ref, l0_ref, l1_ref, l2_ref, expand_ref, w_o_ref,
                     g_ref, b_ref, out_ref):
    lses = [l0_ref[...], l1_ref[...], l2_ref[...]]
    m = jnp.maximum(jnp.maximum(lses[0], lses[1]), lses[2])
    es = [jnp.exp(l - m) for l in lses]
    inv = 1.0 / (es[0] + es[1] + es[2])
    comb = None
    for e, o_ref in zip(es, (o0_ref, o1_ref, o2_ref)):
        w = e * inv
        w_hi = w.astype(BF16)
        w_lo = (w - w_hi.astype(F32)).astype(BF16)
        w_full = _dot(w_hi, expand_ref[...]) + _dot(w_lo, expand_ref[...])
        term = w_full * o_ref[...].astype(F32)
        comb = term if comb is None else comb + term
    mix = _dot(comb.astype(BF16), w_o_ref[...])
    z = DEEPNORM_ALPHA * x_ref[...] + mix
    out_ref[...] = _layer_norm(z, g_ref[...], b_ref[...])


def _attn_out_ln(x, outs, lses, w_o, g, b):
    tm = ROW_TILE
    head_of_lane = jnp.arange(D_MODEL, dtype=jnp.int32) // HEAD_DIM
    expand = (jnp.arange(LSE_LANES, dtype=jnp.int32)[:, None] == head_of_lane[None, :]).astype(BF16)
    row = lambda width: pl.BlockSpec((tm, width), lambda i: (i, 0))
    return pl.pallas_call(
        _attn_out_kernel,
        grid=(TOKENS // tm,),
        in_specs=[row(D_MODEL)] * 4 + [row(LSE_LANES)] * 3 +
                 [_resident((LSE_LANES, D_MODEL)), _resident((D_MODEL, D_MODEL)),
                  _resident((1, D_MODEL)), _resident((1, D_MODEL))],
        out_specs=row(D_MODEL),
        out_shape=jax.ShapeDtypeStruct((TOKENS, D_MODEL), F32),
        compiler_params=_params(1),
        name="attn_out_ln",
    )(x, *outs, *lses, expand, w_o, g, b)


def kernel(x, ln_g, ln_b, ffn_w_in, ffn_w_out, rg_w_in, rg_conv_w, rg_conv_b, rg_gate_w, rg_gate_b,
           rg_lam, rg_w_out, kv_w, attn_w_q, attn_w_o):
    assert x.shape == (BATCH, SEQ, D_MODEL)
    ln = lambda layer, k: (ln_g[layer, k].reshape(1, D_MODEL), ln_b[layer, k].reshape(1, D_MODEL))
    ffn = lambda layer, k: (ffn_w_in[layer, k].astype(BF16), ffn_w_out[layer, k].astype(BF16))

    h = _ffn_ln(x, *ffn(0, 0), *ln(0, 0), in_bsd=True)
    h = _rglru_ln(h, rg_w_in[0].astype(BF16), rg_conv_w[0], rg_conv_b[0].reshape(1, D_RNN),
                  rg_gate_w[0].astype(BF16), rg_gate_b[0], rg_lam[0].reshape(1, D_RNN),
                  rg_w_out[0].astype(BF16), *ln(0, 1))
    h = _ffn_ln(h, *ffn(0, 1), *ln(0, 2))
    k_sh, v_sh = _proj(h, kv_w.astype(BF16), 2)

    h = _ffn_ln(h, *ffn(1, 0), *ln(1, 0))
    qs = _proj(h, attn_w_q[0].astype(BF16), N_GROUPS, scale=1.0 / math.sqrt(HEAD_DIM))
    outs, lses = [], []
    for q_g, (_, dilation) in zip(qs, DILATION_PAIRS):
        o_g, l_g = _attention_group(q_g, k_sh, v_sh, dilation)
        outs.append(o_g)
        lses.append(l_g)
    h = _attn_out_ln(h, outs, lses, attn_w_o[0].astype(BF16), *ln(1, 1))
    return _ffn_ln(h, *ffn(1, 1), *ln(1, 2), out_bsd=True)
```

```python
import functools
import math

import jax
import jax.numpy as jnp
from jax import lax
from jax.experimental import pallas as pl
from jax.experimental.pallas import tpu as pltpu

D_MODEL = 1024
BATCH = 8
SEQ = 4096
DEPTH = 2
D_FF = 2816
D_RNN = 1280
N_GATE_BLOCKS = 5
GATE_BW = D_RNN // N_GATE_BLOCKS
CONV_WIDTH = 4
LRU_C = 8.0
N_HEADS = 16
HEAD_DIM = D_MODEL // N_HEADS
DILATION_PAIRS = ((128, 1), (512, 4), (2048, 16))
DILATIONS = tuple(d for _, d in DILATION_PAIRS)
N_GROUPS = len(DILATION_PAIRS)
DEEPNORM_ALPHA = float((2 * DEPTH) ** 0.25)
LN_EPS = 1e-5
NEG_INF = -1e30

TOKENS = BATCH * SEQ
ROW_TILE = 512
TILES_PER_SEQ = SEQ // ROW_TILE
TIME_TILE = ROW_TILE // BATCH
FF_CHUNK = 256
ATT_BLK = 128
LANES = 128
PAIR = 2 * HEAD_DIM
VMEM_LIMIT = 56 * 1024 * 1024

assert all(w // d == ATT_BLK for w, d in DILATION_PAIRS)
assert PAIR == LANES

BF16 = jnp.bfloat16
F32 = jnp.float32


def _dot(a, b):
    return jnp.dot(a, b, preferred_element_type=F32)


def _dot_nt(a, b):
    return lax.dot_general(a, b, (((1,), (1,)), ((), ())), preferred_element_type=F32)


def _sigmoid(x):
    return 0.5 * jnp.tanh(0.5 * x) + 0.5


def _gelu_tanh(x):
    c = math.sqrt(2.0 / math.pi)
    return 0.5 * x * (1.0 + jnp.tanh(c * (x + 0.044715 * (x * x * x))))


def _layer_norm(z, g, b):
    mu = jnp.mean(z, axis=-1, keepdims=True)
    zc = z - mu
    var = jnp.mean(zc * zc, axis=-1, keepdims=True)
    return zc * lax.rsqrt(var + LN_EPS) * g + b


def _store_lane_tiles(tile_ref, value):
    for c in range(tile_ref.shape[0]):
        tile_ref[c] = value[:, c * LANES:(c + 1) * LANES]


def _resident(shape):
    nd = len(shape)
    return pl.BlockSpec(shape, lambda *_: (0,) * nd, pipeline_mode=pl.Buffered(1))


def _row_spec(width):
    return pl.BlockSpec((ROW_TILE, width), lambda i: (i, 0))


def _phase_spec(dilation, width):
    return pl.BlockSpec((None, dilation, ROW_TILE // dilation, width),
                        lambda i: (i // TILES_PER_SEQ, 0, i % TILES_PER_SEQ, 0))


def _params(n_grid):
    return pltpu.CompilerParams(dimension_semantics=("arbitrary",) * n_grid,
                                vmem_limit_bytes=VMEM_LIMIT)


def _ffn_kernel(x_ref, w_in_ref, w_out_ref, g_ref, b_ref, o_ref, act_ref):
    x = x_ref[...]
    xb = x.astype(BF16)
    for c in range(D_FF // FF_CHUNK):
        lo = c * FF_CHUNK
        gate = _dot(xb, w_in_ref[:, lo:lo + FF_CHUNK])
        up = _dot(xb, w_in_ref[:, D_FF + lo:D_FF + lo + FF_CHUNK])
        act_ref[:, lo:lo + FF_CHUNK] = (gate * _sigmoid(gate) * up).astype(BF16)
    y = _dot(act_ref[...], w_out_ref[...])
    z = DEEPNORM_ALPHA * x + 0.5 * y
    o_ref[...] = _layer_norm(z, g_ref[...], b_ref[...])


def _ffn_ln(x, w_in, w_out, g, b):
    return pl.pallas_call(
        _ffn_kernel,
        grid=(TOKENS // ROW_TILE,),
        in_specs=[_row_spec(D_MODEL), _resident((D_MODEL, 2 * D_FF)), _resident((D_FF, D_MODEL)),
                  _resident((1, D_MODEL)), _resident((1, D_MODEL))],
        out_specs=_row_spec(D_MODEL),
        out_shape=jax.ShapeDtypeStruct((TOKENS, D_MODEL), F32),
        scratch_shapes=[pltpu.VMEM((ROW_TILE, D_FF), BF16)],
        compiler_params=_params(1),
        name="ffn_ln",
    )(x, w_in, w_out, g, b)


def _rglru_kernel(x_ref, w_in_ref, conv_w_ref, conv_b_ref, gate_w_ref, gate_b_ref, lam_ref,
                  w_out_ref, g_ref, b_ref, o_ref, stage_ref, xs_ref, ubuf_ref, a_ref, h_ref, carry_ref):
    tm = ROW_TILE
    halo = (CONV_WIDTH - 1) * BATCH

    @pl.when(pl.program_id(0) == 0)
    def _():
        ubuf_ref[0:halo, :] = jnp.zeros((halo, D_RNN), F32)
        carry_ref[...] = jnp.zeros((BATCH, D_RNN), F32)

    for bb in range(BATCH):
        for c in range(D_MODEL // LANES):
            stage_ref[c, bb * TIME_TILE:(bb + 1) * TIME_TILE, :] = x_ref[bb, :, c * LANES:(c + 1) * LANES]
    for t in range(TIME_TILE):
        for c in range(D_MODEL // LANES):
            xs_ref[t * BATCH:(t + 1) * BATCH, c * LANES:(c + 1) * LANES] = (
                stage_ref[c, pl.ds(t, BATCH, stride=TIME_TILE), :])

    x = xs_ref[...]
    xb = x.astype(BF16)
    y_br = _dot(xb, w_in_ref[:, 0:D_RNN])
    u_br = _dot(xb, w_in_ref[:, D_RNN:2 * D_RNN])

    ubuf_ref[halo:halo + tm, :] = u_br
    u = conv_b_ref[...] + conv_w_ref[CONV_WIDTH - 1:CONV_WIDTH, :] * u_br
    for j in range(CONV_WIDTH - 1):
        u = u + conv_w_ref[j:j + 1, :] * ubuf_ref[j * BATCH:j * BATCH + tm, :]
    ubuf_ref[0:halo, :] = u_br[tm - halo:tm, :]

    ub = u.astype(BF16)
    rec_parts, in_parts = [], []
    for n in range(N_GATE_BLOCKS):
        blk = ub[:, n * GATE_BW:(n + 1) * GATE_BW]
        rec_parts.append(_dot(blk, gate_w_ref[0, n]))
        in_parts.append(_dot(blk, gate_w_ref[1, n]))
    rec_gate = _sigmoid(jnp.concatenate(rec_parts, axis=1) + gate_b_ref[0:1, :])
    in_gate = _sigmoid(jnp.concatenate(in_parts, axis=1) + gate_b_ref[1:2, :])

    neg_lam = -lam_ref[...]
    softplus = jnp.maximum(neg_lam, 0.0) + jnp.log1p(jnp.exp(-jnp.abs(neg_lam)))
    log_a = (-LRU_C) * rec_gate * softplus
    a = jnp.exp(log_a)
    a_ref[...] = a
    h_ref[...] = jnp.sqrt(1.0 - a * a) * (in_gate * u)

    def step(t, h):
        r = pl.multiple_of(t * BATCH, BATCH)
        h = a_ref[pl.ds(r, BATCH), :] * h + h_ref[pl.ds(r, BATCH), :]
        h_ref[pl.ds(r, BATCH), :] = h
        return h

    carry_ref[...] = lax.fori_loop(0, TIME_TILE, step, carry_ref[...], unroll=8)

    mixed = (_gelu_tanh(y_br) * h_ref[...]).astype(BF16)
    z = DEEPNORM_ALPHA * x + _dot(mixed, w_out_ref[...])
    _store_lane_tiles(stage_ref, _layer_norm(z, g_ref[...], b_ref[...]))
    for bb in range(BATCH):
        for c in range(D_MODEL // LANES):
            o_ref[bb, :, c * LANES:(c + 1) * LANES] = stage_ref[c, pl.ds(bb, TIME_TILE, stride=BATCH), :]


def _rglru_ln(x, w_in, conv_w, conv_b, gate_w, gate_b, lam, w_out, g, b):
    tm = ROW_TILE
    halo = (CONV_WIDTH - 1) * BATCH
    blk = pl.BlockSpec((BATCH, TIME_TILE, D_MODEL), lambda i: (0, i, 0))
    return pl.pallas_call(
        _rglru_kernel,
        grid=(SEQ // TIME_TILE,),
        in_specs=[blk,
                  _resident((D_MODEL, 2 * D_RNN)), _resident((CONV_WIDTH, D_RNN)),
                  _resident((1, D_RNN)), _resident((2, N_GATE_BLOCKS, GATE_BW, GATE_BW)),
                  _resident((2, D_RNN)), _resident((1, D_RNN)), _resident((D_RNN, D_MODEL)),
                  _resident((1, D_MODEL)), _resident((1, D_MODEL))],
        out_specs=blk,
        out_shape=jax.ShapeDtypeStruct((BATCH, SEQ, D_MODEL), F32),
        scratch_shapes=[pltpu.VMEM((D_MODEL // LANES, tm, LANES), F32), pltpu.VMEM((tm, D_MODEL), F32),
                        pltpu.VMEM((tm + halo, D_RNN), F32), pltpu.VMEM((tm, D_RNN), F32),
                        pltpu.VMEM((tm, D_RNN), F32), pltpu.VMEM((BATCH, D_RNN), F32)],
        compiler_params=_params(1),
        name="rglru_ln",
    )(x, w_in, conv_w, conv_b, gate_w, gate_b, lam, w_out, g, b)


def _proj_kernel(x_ref, w_ref, *refs, plan, scale):
    o_refs, y_ref = refs[:-1], refs[-1]
    y = _dot(x_ref[...].astype(BF16), w_ref[...])
    if scale != 1.0:
        y = y * scale
    _store_lane_tiles(y_ref, y)
    tiles = D_MODEL // LANES
    for (cb, dilation), o_ref in zip(plan, o_refs):
        rows = ROW_TILE // dilation
        for p in range(dilation):
            for c in range(tiles):
                if dilation == 1:
                    piece = y_ref[cb * tiles + c]
                else:
                    piece = y_ref[cb * tiles + c, pl.ds(p, rows, stride=dilation), :]
                o_ref[p, :, c * LANES:(c + 1) * LANES] = piece.astype(BF16)


def _proj(x, w, plan, scale=1.0):
    n_cols = w.shape[1]
    return pl.pallas_call(
        functools.partial(_proj_kernel, plan=plan, scale=scale),
        grid=(TOKENS // ROW_TILE,),
        in_specs=[_row_spec(D_MODEL), _resident((D_MODEL, n_cols))],
        out_specs=[_phase_spec(d, D_MODEL) for _, d in plan],
        out_shape=[jax.ShapeDtypeStruct((BATCH, d, SEQ // d, D_MODEL), BF16) for _, d in plan],
        scratch_shapes=[pltpu.VMEM((n_cols // LANES, ROW_TILE, LANES), F32)],
        compiler_params=_params(1),
        name=f"proj{len(plan)}",
    )(x, w)


def _attn_kernel(q_ref, kp_ref, kc_ref, vp_ref, vc_ref, o_ref, lse_ref, bias_ref, *, dilation, qrows):
    is_first_step = ((pl.program_id(0) == 0) & (pl.program_id(1) == 0) & (pl.program_id(2) == 0))

    @pl.when(is_first_step)
    def _():
        row = lax.broadcasted_iota(jnp.int32, (ATT_BLK, 2 * ATT_BLK), 0)
        col = lax.broadcasted_iota(jnp.int32, (ATT_BLK, 2 * ATT_BLK), 1)
        dist = row + ATT_BLK - col
        valid = (dist >= 0) & (dist <= ATT_BLK)
        dist_f = dist.astype(F32)
        for head in range(N_HEADS):
            slope = 2.0 ** (-8.0 * (head + 1) / N_HEADS)
            bias_ref[head] = jnp.where(valid, dist_f * (-slope * dilation), NEG_INF)

    lane = lax.broadcasted_iota(jnp.int32, (1, LANES), 1)
    lane_lo = lane < HEAD_DIM
    sels = (lane_lo, jnp.logical_not(lane_lo))
    zero = jnp.zeros((), BF16)
    ones = [jnp.broadcast_to(jnp.where(s, 1.0, 0.0).astype(BF16), (2 * ATT_BLK, LANES)) for s in sels]
    col2 = lax.broadcasted_iota(jnp.int32, (1, 2 * ATT_BLK), 1)
    no_prev = jnp.where(col2 < ATT_BLK, jnp.where(pl.program_id(2) == 0, NEG_INF, 0.0), 0.0)

    for j in range(qrows // ATT_BLK):
        r0 = j * ATT_BLK
        lse_tile = jnp.zeros((ATT_BLK, LANES), F32)
        for pair in range(N_HEADS // 2):
            lanes = slice(pair * PAIR, (pair + 1) * PAIR)
            q2 = q_ref[r0:r0 + ATT_BLK, lanes]
            if j == 0:
                kw = jnp.concatenate([kp_ref[:, lanes], kc_ref[0:ATT_BLK, lanes]], axis=0)
                vw = jnp.concatenate([vp_ref[:, lanes], vc_ref[0:ATT_BLK, lanes]], axis=0)
            else:
                kw = kc_ref[r0 - ATT_BLK:r0 + ATT_BLK, lanes]
                vw = vc_ref[r0 - ATT_BLK:r0 + ATT_BLK, lanes]
            probs, maxes, v_rows = [], [], []
            for e in range(2):
                s = _dot_nt(q2, jnp.where(sels[e], kw, zero)) + bias_ref[2 * pair + e]
                if j == 0:
                    s = s + no_prev
                m = jnp.max(s, axis=-1, keepdims=True)
                probs.append(jnp.exp(s - m).astype(BF16))
                maxes.append(m)
                v_rows.append(jnp.concatenate([jnp.where(sels[e], vw, zero), ones[e]], axis=1))
            acc = _dot(jnp.concatenate(probs, axis=1), jnp.concatenate(v_rows, axis=0))
            denom = acc[:, LANES:]
            o_ref[r0:r0 + ATT_BLK, lanes] = (acc[:, :LANES] * (1.0 / denom)).astype(BF16)
            lse_pair = jnp.where(lane_lo, maxes[0], maxes[1]) + jnp.log(denom)
            keep = (lane == pair) | (lane == HEAD_DIM + pair)
            lse_tile = jnp.where(keep, lse_pair, lse_tile)
        lse_ref[r0:r0 + ATT_BLK, :] = lse_tile


def _attention_group(q, k, v, dilation):
    s_d = SEQ // dilation
    qrows = min(512, s_d)
    ratio = qrows // ATT_BLK
    cur = pl.BlockSpec((None, None, qrows, D_MODEL), lambda bb, p, t: (bb, p, t, 0))
    prev = pl.BlockSpec((None, None, ATT_BLK, D_MODEL),
                        lambda bb, p, t: (bb, p, jnp.maximum(t * ratio - 1, 0), 0))
    return pl.pallas_call(
        functools.partial(_attn_kernel, dilation=dilation, qrows=qrows),
        grid=(BATCH, dilation, s_d // qrows),
        in_specs=[cur, prev, cur, prev, cur],
        out_specs=[cur, pl.BlockSpec((None, None, qrows, LANES), lambda bb, p, t: (bb, p, t, 0))],
        out_shape=[jax.ShapeDtypeStruct((BATCH, dilation, s_d, D_MODEL), BF16),
                   jax.ShapeDtypeStruct((BATCH, dilation, s_d, LANES), F32)],
        scratch_shapes=[pltpu.VMEM((N_HEADS, ATT_BLK, 2 * ATT_BLK), F32)],
        compiler_params=_params(3),
        name=f"attn_d{dilation}",
    )(q, k, k, v, v)


def _attn_out_kernel(x_ref, o0_ref, o1_ref, o2_ref, lse_ref, expand_ref, w_o_ref, g_ref, b_ref,
                     out_ref, nat_ref):
    lse = lse_ref[...]
    lses = [lse] + [pltpu.roll(lse, LANES - N_HEADS * g, axis=1) for g in range(1, N_GROUPS)]
    m = jnp.maximum(jnp.maximum(lses[0], lses[1]), lses[2])
    es = [jnp.exp(l - m) for l in lses]
    inv = 1.0 / (es[0] + es[1] + es[2])
    comb = None
    for e, o_ref, dilation in zip(es, (o0_ref, o1_ref, o2_ref), DILATIONS):
        w = e * inv
        w_hi = w.astype(BF16)
        w_lo = (w - w_hi.astype(F32)).astype(BF16)
        w_full = _dot(w_hi, expand_ref[...]) + _dot(w_lo, expand_ref[...])
        if dilation == 1:
            o_nat = o_ref[0].astype(F32)
        else:
            rows = ROW_TILE // dilation
            tiles = D_MODEL // LANES
            for p in range(dilation):
                for c in range(tiles):
                    nat_ref[c, pl.ds(p, rows, stride=dilation), :] = (
                        o_ref[p, :, c * LANES:(c + 1) * LANES].astype(F32))
            o_nat = jnp.concatenate([nat_ref[c] for c in range(tiles)], axis=1)
        term = w_full * o_nat
        comb = term if comb is None else comb + term
    mix = _dot(comb.astype(BF16), w_o_ref[...])
    z = DEEPNORM_ALPHA * x_ref[...] + mix
    out_ref[...] = _layer_norm(z, g_ref[...], b_ref[...])


def _attn_out_ln(x, outs, lse_nat, w_o, g, b):
    slot = jnp.arange(LANES, dtype=jnp.int32)
    head_of_slot = jnp.where(slot < N_HEADS // 2, 2 * slot, 2 * (slot - N_HEADS // 2) + 1)
    head_of_lane = jnp.arange(D_MODEL, dtype=jnp.int32) // HEAD_DIM
    expand = ((head_of_slot[:, None] == head_of_lane[None, :]) & (slot[:, None] < N_HEADS)).astype(BF16)
    return pl.pallas_call(
        _attn_out_kernel,
        grid=(TOKENS // ROW_TILE,),
        in_specs=[_row_spec(D_MODEL)] + [_phase_spec(d, D_MODEL) for d in DILATIONS] +
                 [_row_spec(LANES), _resident((LANES, D_MODEL)), _resident((D_MODEL, D_MODEL)),
                  _resident((1, D_MODEL)), _resident((1, D_MODEL))],
        out_specs=_row_spec(D_MODEL),
        out_shape=jax.ShapeDtypeStruct((TOKENS, D_MODEL), F32),
        scratch_shapes=[pltpu.VMEM((D_MODEL // LANES, ROW_TILE, LANES), F32)],
        compiler_params=_params(1),
        name="attn_out_ln",
    )(x, *outs, lse_nat, expand, w_o, g, b)


def _lse_to_natural(lses):
    half = N_HEADS // 2
    parts = []
    for l, d in zip(lses, DILATIONS):
        l = jnp.concatenate([l[..., :half], l[..., HEAD_DIM:HEAD_DIM + half]], axis=-1)
        parts.append(l.transpose(0, 2, 1, 3).reshape(TOKENS, N_HEADS))
    parts.append(jnp.zeros((TOKENS, LANES - N_GROUPS * N_HEADS), F32))
    return jnp.concatenate(parts, axis=-1)


def kernel(x, ln_g, ln_b, ffn_w_in, ffn_w_out, rg_w_in, rg_conv_w, rg_conv_b, rg_gate_w, rg_gate_b,
           rg_lam, rg_w_out, kv_w, attn_w_q, attn_w_o):
    assert x.shape == (BATCH, SEQ, D_MODEL)
    ln = lambda layer, k: (ln_g[layer, k].reshape(1, D_MODEL), ln_b[layer, k].reshape(1, D_MODEL))
    ffn = lambda layer, k: (ffn_w_in[layer, k].astype(BF16), ffn_w_out[layer, k].astype(BF16))

    h = _ffn_ln(x.reshape(TOKENS, D_MODEL), *ffn(0, 0), *ln(0, 0))
    h = _rglru_ln(h.reshape(BATCH, SEQ, D_MODEL), rg_w_in[0].astype(BF16), rg_conv_w[0],
                  rg_conv_b[0].reshape(1, D_RNN), rg_gate_w[0].astype(BF16), rg_gate_b[0],
                  rg_lam[0].reshape(1, D_RNN), rg_w_out[0].astype(BF16), *ln(0, 1))
    h = _ffn_ln(h.reshape(TOKENS, D_MODEL), *ffn(0, 1), *ln(0, 2))
    kv = _proj(h, kv_w.astype(BF16), [(c, d) for d in DILATIONS for c in range(2)])

    h = _ffn_ln(h, *ffn(1, 0), *ln(1, 0))
    qs = _proj(h, attn_w_q[0].astype(BF16), [(g, d) for g, d in enumerate(DILATIONS)],
               scale=1.0 / math.sqrt(HEAD_DIM))
    outs, lses = [], []
    for g, dilation in enumerate(DILATIONS):
        o_g, l_g = _attention_group(qs[g], kv[2 * g], kv[2 * g + 1], dilation)
        outs.append(o_g)
        lses.append(l_g)
    h = _attn_out_ln(h, outs, _lse_to_natural(lses), attn_w_o[0].astype(BF16), *ln(1, 1))
    h = _ffn_ln(h, *ffn(1, 1), *ln(1, 2))
    return h.reshape(BATCH, SEQ, D_MODEL)
```

```python
import functools
import math

import jax
import jax.numpy as jnp
from jax import lax
from jax.experimental import pallas as pl
from jax.experimental.pallas import tpu as pltpu

D_MODEL = 1024
BATCH = 8
SEQ = 4096
DEPTH = 2
D_FF = 2816
D_RNN = 1280
N_GATE_BLOCKS = 5
GATE_BW = D_RNN // N_GATE_BLOCKS
CONV_WIDTH = 4
LRU_C = 8.0
N_HEADS = 16
HEAD_DIM = D_MODEL // N_HEADS
DILATION_PAIRS = ((128, 1), (512, 4), (2048, 16))
DILATIONS = tuple(d for _, d in DILATION_PAIRS)
N_GROUPS = len(DILATION_PAIRS)
DEEPNORM_ALPHA = float((2 * DEPTH) ** 0.25)
LN_EPS = 1e-5
NEG_INF = -1e30

TOKENS = BATCH * SEQ
ROW_TILE = 512
TILES_PER_SEQ = SEQ // ROW_TILE
FFN_SUB_TILE = 512
FFN_SUB_TILES = 2
TIME_TILE = 64
RG_ROWS = TIME_TILE * BATCH
FF_CHUNK = 256
ATT_BLK = 128
LANES = 128
D_TILES = D_MODEL // LANES
PAIR = 2 * HEAD_DIM
PHASE_STEP = 4
STAGE_PITCH = TIME_TILE + 8
VMEM_LIMIT = 56 * 1024 * 1024

assert all(w // d == ATT_BLK for w, d in DILATION_PAIRS)
assert DILATIONS == (1, PHASE_STEP, PHASE_STEP * PHASE_STEP)
assert PAIR == LANES

BF16 = jnp.bfloat16
F32 = jnp.float32


def _dot(a, b):
    return jnp.dot(a, b, preferred_element_type=F32)


def _dot_nt(a, b):
    return lax.dot_general(a, b, (((1,), (1,)), ((), ())), preferred_element_type=F32)


def _sigmoid(x):
    return 0.5 * jnp.tanh(0.5 * x) + 0.5


def _gelu_tanh(x):
    c = math.sqrt(2.0 / math.pi)
    return 0.5 * x * (1.0 + jnp.tanh(c * (x + 0.044715 * (x * x * x))))


def _layer_norm(z, g, b):
    mu = jnp.mean(z, axis=-1, keepdims=True)
    zc = z - mu
    var = jnp.mean(zc * zc, axis=-1, keepdims=True)
    return zc * lax.rsqrt(var + LN_EPS) * g + b


def _lane_tile(c):
    return slice(c * LANES, (c + 1) * LANES)


def _store_lane_tiles(tile_ref, value):
    for c in range(tile_ref.shape[0]):
        tile_ref[c] = value[:, _lane_tile(c)]


def _resident(shape):
    nd = len(shape)
    return pl.BlockSpec(shape, lambda *_: (0,) * nd, pipeline_mode=pl.Buffered(1))


def _row_spec(width, rows=ROW_TILE):
    return pl.BlockSpec((rows, width), lambda i: (i, 0))


def _phase_spec(dilation, width):
    return pl.BlockSpec((None, dilation, ROW_TILE // dilation, width),
                        lambda i: (i // TILES_PER_SEQ, 0, i % TILES_PER_SEQ, 0))


def _params(n_grid):
    return pltpu.CompilerParams(dimension_semantics=("arbitrary",) * n_grid,
                                vmem_limit_bytes=VMEM_LIMIT)


def _ffn_kernel(x_ref, w_in_ref, w_out_ref, g_ref, b_ref, o_ref, act_ref):
    for t in range(FFN_SUB_TILES):
        rows = slice(t * FFN_SUB_TILE, (t + 1) * FFN_SUB_TILE)
        x = x_ref[rows, :]
        xb = x.astype(BF16)
        for c in range(D_FF // FF_CHUNK):
            lo = c * FF_CHUNK
            gate = _dot(xb, w_in_ref[:, lo:lo + FF_CHUNK])
            up = _dot(xb, w_in_ref[:, D_FF + lo:D_FF + lo + FF_CHUNK])
            act_ref[t, :, lo:lo + FF_CHUNK] = (gate * _sigmoid(gate) * up).astype(BF16)
        y = _dot(act_ref[t], w_out_ref[...])
        z = DEEPNORM_ALPHA * x + 0.5 * y
        o_ref[rows, :] = _layer_norm(z, g_ref[...], b_ref[...])


def _ffn_ln(x, w_in, w_out, g, b):
    rows = FFN_SUB_TILE * FFN_SUB_TILES
    return pl.pallas_call(
        _ffn_kernel,
        grid=(TOKENS // rows,),
        in_specs=[_row_spec(D_MODEL, rows), _resident((D_MODEL, 2 * D_FF)), _resident((D_FF, D_MODEL)),
                  _resident((1, D_MODEL)), _resident((1, D_MODEL))],
        out_specs=_row_spec(D_MODEL, rows),
        out_shape=jax.ShapeDtypeStruct((TOKENS, D_MODEL), F32),
        scratch_shapes=[pltpu.VMEM((FFN_SUB_TILES, FFN_SUB_TILE, D_FF), BF16)],
        compiler_params=_params(1),
        name="ffn_ln",
    )(x, w_in, w_out, g, b)


def _rglru_kernel(x_ref, w_in_ref, conv_w_ref, conv_b_ref, gate_w_ref, gate_b_ref, lam_ref,
                  w_out_ref, g_ref, b_ref, o_ref, stage_in_ref, stage_out_ref, xs_ref, ubuf_ref,
                  a_ref, h_ref, carry_ref):
    tm = RG_ROWS
    halo = (CONV_WIDTH - 1) * BATCH

    @pl.when(pl.program_id(0) == 0)
    def _():
        ubuf_ref[0:halo, :] = jnp.zeros((halo, D_RNN), F32)
        carry_ref[...] = jnp.zeros((BATCH, D_RNN), F32)

    for bb in range(BATCH):
        for c in range(D_TILES):
            stage_in_ref[c, bb * STAGE_PITCH:bb * STAGE_PITCH + TIME_TILE, :] = x_ref[bb, :, _lane_tile(c)]
    for t in range(TIME_TILE):
        for c in range(D_TILES):
            xs_ref[t * BATCH:(t + 1) * BATCH, _lane_tile(c)] = (
                stage_in_ref[c, pl.ds(t, BATCH, stride=STAGE_PITCH), :])

    x = xs_ref[...]
    xb = x.astype(BF16)
    y_br = _dot(xb, w_in_ref[:, 0:D_RNN])
    u_br = _dot(xb, w_in_ref[:, D_RNN:2 * D_RNN])

    ubuf_ref[halo:halo + tm, :] = u_br
    u = conv_b_ref[...] + conv_w_ref[CONV_WIDTH - 1:CONV_WIDTH, :] * u_br
    for j in range(CONV_WIDTH - 1):
        u = u + conv_w_ref[j:j + 1, :] * ubuf_ref[j * BATCH:j * BATCH + tm, :]
    ubuf_ref[0:halo, :] = u_br[tm - halo:tm, :]

    ub = u.astype(BF16)
    rec_parts, in_parts = [], []
    for n in range(N_GATE_BLOCKS):
        blk = ub[:, n * GATE_BW:(n + 1) * GATE_BW]
        rec_parts.append(_dot(blk, gate_w_ref[0, n]))
        in_parts.append(_dot(blk, gate_w_ref[1, n]))
    rec_gate = _sigmoid(jnp.concatenate(rec_parts, axis=1) + gate_b_ref[0:1, :])
    in_gate = _sigmoid(jnp.concatenate(in_parts, axis=1) + gate_b_ref[1:2, :])

    neg_lam = -lam_ref[...]
    softplus = jnp.maximum(neg_lam, 0.0) + jnp.log1p(jnp.exp(-jnp.abs(neg_lam)))
    log_a = (-LRU_C) * rec_gate * softplus
    a = jnp.exp(log_a)
    one_m_a2 = 1.0 - a * a
    a_ref[...] = a
    h_ref[...] = (one_m_a2 * lax.rsqrt(jnp.maximum(one_m_a2, 1e-30))) * (in_gate * u)

    def step(t, h):
        r = pl.multiple_of(t * BATCH, BATCH)
        h = a_ref[pl.ds(r, BATCH), :] * h + h_ref[pl.ds(r, BATCH), :]
        h_ref[pl.ds(r, BATCH), :] = h
        return h

    carry_ref[...] = lax.fori_loop(0, TIME_TILE, step, carry_ref[...], unroll=8)

    mixed = (_gelu_tanh(y_br) * h_ref[...]).astype(BF16)
    z = DEEPNORM_ALPHA * x + _dot(mixed, w_out_ref[...])
    _store_lane_tiles(stage_out_ref, _layer_norm(z, g_ref[...], b_ref[...]))
    for bb in range(BATCH):
        for c in range(D_TILES):
            o_ref[bb, :, _lane_tile(c)] = stage_out_ref[c, pl.ds(bb, TIME_TILE, stride=BATCH), :]


def _rglru_ln(x, w_in, conv_w, conv_b, gate_w, gate_b, lam, w_out, g, b):
    tm = RG_ROWS
    halo = (CONV_WIDTH - 1) * BATCH
    blk = pl.BlockSpec((BATCH, TIME_TILE, D_MODEL), lambda i: (0, i, 0))
    return pl.pallas_call(
        _rglru_kernel,
        grid=(SEQ // TIME_TILE,),
        in_specs=[blk,
                  _resident((D_MODEL, 2 * D_RNN)), _resident((CONV_WIDTH, D_RNN)),
                  _resident((1, D_RNN)), _resident((2, N_GATE_BLOCKS, GATE_BW, GATE_BW)),
                  _resident((2, D_RNN)), _resident((1, D_RNN)), _resident((D_RNN, D_MODEL)),
                  _resident((1, D_MODEL)), _resident((1, D_MODEL))],
        out_specs=blk,
        out_shape=jax.ShapeDtypeStruct((BATCH, SEQ, D_MODEL), F32),
        scratch_shapes=[pltpu.VMEM((D_TILES, BATCH * STAGE_PITCH, LANES), F32),
                        pltpu.VMEM((D_TILES, tm, LANES), F32), pltpu.VMEM((tm, D_MODEL), F32),
                        pltpu.VMEM((tm + halo, D_RNN), F32), pltpu.VMEM((tm, D_RNN), F32),
                        pltpu.VMEM((tm, D_RNN), F32), pltpu.VMEM((BATCH, D_RNN), F32)],
        compiler_params=_params(1),
        name="rglru_ln",
    )(x, w_in, conv_w, conv_b, gate_w, gate_b, lam, w_out, g, b)


def _proj_kernel(x_ref, w_ref, *refs, plan, scale):
    n_out = sum(len(dils) for _, dils in plan)
    o_refs, scratch = list(refs[:n_out]), refs[n_out:]
    xb = x_ref[...].astype(BF16)
    quarter = ROW_TILE // PHASE_STEP
    for k, (cb, dils) in enumerate(plan):
        y = _dot(xb, w_ref[:, cb * D_MODEL:(cb + 1) * D_MODEL])
        if scale != 1.0:
            y = y * scale
        outs = {d: o_refs.pop(0) for d in dils}
        nat_ref, ph_ref = scratch[2 * k], scratch[2 * k + 1]
        if 1 in outs:
            outs[1][0] = y.astype(BF16)
        if all(d == 1 for d in outs):
            continue
        _store_lane_tiles(nat_ref, y)
        for c in range(D_TILES):
            for p in range(PHASE_STEP):
                v = nat_ref[c, pl.ds(p, quarter, stride=PHASE_STEP), :]
                if PHASE_STEP in outs:
                    outs[PHASE_STEP][p, :, _lane_tile(c)] = v.astype(BF16)
                if PHASE_STEP ** 2 in outs:
                    ph_ref[c, p * quarter:(p + 1) * quarter, :] = v
        if PHASE_STEP ** 2 in outs:
            sub = quarter // PHASE_STEP
            for c in range(D_TILES):
                for p in range(PHASE_STEP):
                    for j in range(PHASE_STEP):
                        v = ph_ref[c, pl.ds(p * quarter + j, sub, stride=PHASE_STEP), :]
                        outs[PHASE_STEP ** 2][p + PHASE_STEP * j, :, _lane_tile(c)] = v.astype(BF16)


def _proj(x, w, plan, scale=1.0):
    dil_list = [d for _, dils in plan for d in dils]
    tile_scratch = pltpu.VMEM((D_TILES, ROW_TILE, LANES), F32)
    return pl.pallas_call(
        functools.partial(_proj_kernel, plan=plan, scale=scale),
        grid=(TOKENS // ROW_TILE,),
        in_specs=[_row_spec(D_MODEL), _resident((D_MODEL, w.shape[1]))],
        out_specs=[_phase_spec(d, D_MODEL) for d in dil_list],
        out_shape=[jax.ShapeDtypeStruct((BATCH, d, SEQ // d, D_MODEL), BF16) for d in dil_list],
        scratch_shapes=[tile_scratch] * (2 * len(plan)),
        compiler_params=_params(1),
        name=f"proj{len(dil_list)}",
    )(x, w)


def _attn_kernel(q_ref, kp_ref, kc_ref, vp_ref, vc_ref, o_ref, lse_ref, bias_ref, *, dilation, qrows):
    is_first_step = ((pl.program_id(0) == 0) & (pl.program_id(1) == 0) & (pl.program_id(2) == 0))

    @pl.when(is_first_step)
    def _():
        row = lax.broadcasted_iota(jnp.int32, (ATT_BLK, 2 * ATT_BLK), 0)
        col = lax.broadcasted_iota(jnp.int32, (ATT_BLK, 2 * ATT_BLK), 1)
        dist = row + ATT_BLK - col
        valid = (dist >= 0) & (dist <= ATT_BLK)
        dist_f = dist.astype(F32)
        for head in range(N_HEADS):
            slope = 2.0 ** (-8.0 * (head + 1) / N_HEADS)
            bias_ref[head] = jnp.where(valid, dist_f * (-slope * dilation), NEG_INF)

    lane = lax.broadcasted_iota(jnp.int32, (1, LANES), 1)
    lane_lo = lane < HEAD_DIM
    sels = (lane_lo, jnp.logical_not(lane_lo))
    zero = jnp.zeros((), BF16)
    ones = [jnp.broadcast_to(jnp.where(s, 1.0, 0.0).astype(BF16), (2 * ATT_BLK, LANES)) for s in sels]
    col2 = lax.broadcasted_iota(jnp.int32, (1, 2 * ATT_BLK), 1)
    no_prev = jnp.where(col2 < ATT_BLK, jnp.where(pl.program_id(2) == 0, NEG_INF, 0.0), 0.0)

    for j in range(qrows // ATT_BLK):
        r0 = j * ATT_BLK
        lse_tile = jnp.zeros((ATT_BLK, LANES), F32)
        for pair in range(N_HEADS // 2):
            lanes = slice(pair * PAIR, (pair + 1) * PAIR)
            q2 = q_ref[r0:r0 + ATT_BLK, lanes]
            if j == 0:
                kw = jnp.concatenate([kp_ref[:, lanes], kc_ref[0:ATT_BLK, lanes]], axis=0)
                vw = jnp.concatenate([vp_ref[:, lanes], vc_ref[0:ATT_BLK, lanes]], axis=0)
            else:
                kw = kc_ref[r0 - ATT_BLK:r0 + ATT_BLK, lanes]
                vw = vc_ref[r0 - ATT_BLK:r0 + ATT_BLK, lanes]
            probs, maxes, v_rows = [], [], []
            for e in range(2):
                s = _dot_nt(q2, jnp.where(sels[e], kw, zero)) + bias_ref[2 * pair + e]
                if j == 0:
                    s = s + no_prev
                m = jnp.max(s, axis=-1, keepdims=True)
                probs.append(jnp.exp(s - m).astype(BF16))
                maxes.append(m)
                v_rows.append(jnp.concatenate([jnp.where(sels[e], vw, zero), ones[e]], axis=1))
            acc = _dot(jnp.concatenate(probs, axis=1), jnp.concatenate(v_rows, axis=0))
            denom = acc[:, LANES:]
            o_ref[r0:r0 + ATT_BLK, lanes] = (acc[:, :LANES] * (1.0 / denom)).astype(BF16)
            lse_pair = jnp.where(lane_lo, maxes[0], maxes[1]) + jnp.log(denom)
            keep = (lane == pair) | (lane == HEAD_DIM + pair)
            lse_tile = jnp.where(keep, lse_pair, lse_tile)
        lse_ref[r0:r0 + ATT_BLK, :] = lse_tile


def _attention_group(q, k, v, dilation):
    s_d = SEQ // dilation
    qrows = min(512, s_d)
    ratio = qrows // ATT_BLK
    cur = pl.BlockSpec((None, None, qrows, D_MODEL), lambda bb, p, t: (bb, p, t, 0))
    prev = pl.BlockSpec((None, None, ATT_BLK, D_MODEL),
                        lambda bb, p, t: (bb, p, jnp.maximum(t * ratio - 1, 0), 0))
    return pl.pallas_call(
        functools.partial(_attn_kernel, dilation=dilation, qrows=qrows),
        grid=(BATCH, dilation, s_d // qrows),
        in_specs=[cur, prev, cur, prev, cur],
        out_specs=[cur, pl.BlockSpec((None, None, qrows, LANES), lambda bb, p, t: (bb, p, t, 0))],
        out_shape=[jax.ShapeDtypeStruct((BATCH, dilation, s_d, D_MODEL), BF16),
                   jax.ShapeDtypeStruct((BATCH, dilation, s_d, LANES), F32)],
        scratch_shapes=[pltpu.VMEM((N_HEADS, ATT_BLK, 2 * ATT_BLK), F32)],
        compiler_params=_params(3),
        name=f"attn_d{dilation}",
    )(q, k, k, v, v)


def _to_natural(src_ref, dilation, nat_ref, ph_ref):
    if dilation == 1:
        return src_ref[0].astype(F32)
    n_tiles = src_ref.shape[-1] // LANES
    quarter = ROW_TILE // PHASE_STEP
    for c in range(n_tiles):
        for p in range(PHASE_STEP):
            if dilation == PHASE_STEP:
                v = src_ref[p, :, _lane_tile(c)].astype(F32)
            else:
                sub = quarter // PHASE_STEP
                for j in range(PHASE_STEP):
                    ph_ref[c, pl.ds(p * quarter + j, sub, stride=PHASE_STEP), :] = (
                        src_ref[p + PHASE_STEP * j, :, _lane_tile(c)].astype(F32))
                v = ph_ref[c, p * quarter:(p + 1) * quarter, :]
            nat_ref[c, pl.ds(p, quarter, stride=PHASE_STEP), :] = v
    return jnp.concatenate([nat_ref[c] for c in range(n_tiles)], axis=1)


def _attn_out_kernel(x_ref, o0_ref, o1_ref, o2_ref, l0_ref, l1_ref, l2_ref, expand_ref, w_o_ref,
                     g_ref, b_ref, out_ref, *scratch):
    o_refs, l_refs = (o0_ref, o1_ref, o2_ref), (l0_ref, l1_ref, l2_ref)
    lses = [_to_natural(l_refs[g], d, scratch[4 * g + 2], scratch[4 * g + 3])
            for g, d in enumerate(DILATIONS)]
    m = jnp.maximum(jnp.maximum(lses[0], lses[1]), lses[2])
    es = [jnp.exp(l - m) for l in lses]
    inv = 1.0 / (es[0] + es[1] + es[2])
    comb = None
    for g, dilation in enumerate(DILATIONS):
        w = es[g] * inv
        w_hi = w.astype(BF16)
        w_lo = (w - w_hi.astype(F32)).astype(BF16)
        w_full = _dot(w_hi, expand_ref[...]) + _dot(w_lo, expand_ref[...])
        term = w_full * _to_natural(o_refs[g], dilation, scratch[4 * g], scratch[4 * g + 1])
        comb = term if comb is None else comb + term
    mix = _dot(comb.astype(BF16), w_o_ref[...])
    z = DEEPNORM_ALPHA * x_ref[...] + mix
    out_ref[...] = _layer_norm(z, g_ref[...], b_ref[...])


def _attn_out_ln(x, outs, lses, w_o, g, b):
    lane = jnp.arange(LANES, dtype=jnp.int32)
    head_of_lane = jnp.where(lane < HEAD_DIM, 2 * lane, 2 * (lane - HEAD_DIM) + 1)
    used = (lane % HEAD_DIM) < N_HEADS // 2
    head_of_col = jnp.arange(D_MODEL, dtype=jnp.int32) // HEAD_DIM
    expand = ((head_of_lane[:, None] == head_of_col[None, :]) & used[:, None]).astype(BF16)
    wide = pltpu.VMEM((D_TILES, ROW_TILE, LANES), F32)
    narrow = pltpu.VMEM((1, ROW_TILE, LANES), F32)
    return pl.pallas_call(
        _attn_out_kernel,
        grid=(TOKENS // ROW_TILE,),
        in_specs=[_row_spec(D_MODEL)] + [_phase_spec(d, D_MODEL) for d in DILATIONS] +
                 [_phase_spec(d, LANES) for d in DILATIONS] +
                 [_resident((LANES, D_MODEL)), _resident((D_MODEL, D_MODEL)),
                  _resident((1, D_MODEL)), _resident((1, D_MODEL))],
        out_specs=_row_spec(D_MODEL),
        out_shape=jax.ShapeDtypeStruct((TOKENS, D_MODEL), F32),
        scratch_shapes=[wide, wide, narrow, narrow] * N_GROUPS,
        compiler_params=_params(1),
        name="attn_out_ln",
    )(x, *outs, *lses, expand, w_o, g, b)


def kernel(x, ln_g, ln_b, ffn_w_in, ffn_w_out, rg_w_in, rg_conv_w, rg_conv_b, rg_gate_w, rg_gate_b,
           rg_lam, rg_w_out, kv_w, attn_w_q, attn_w_o):
    assert x.shape == (BATCH, SEQ, D_MODEL)
    ln = lambda layer, k: (ln_g[layer, k].reshape(1, D_MODEL), ln_b[layer, k].reshape(1, D_MODEL))
    ffn = lambda layer, k: (ffn_w_in[layer, k].astype(BF16), ffn_w_out[layer, k].astype(BF16))

    h = _ffn_ln(x.reshape(TOKENS, D_MODEL), *ffn(0, 0), *ln(0, 0))
    h = _rglru_ln(h.reshape(BATCH, SEQ, D_MODEL), rg_w_in[0].astype(BF16), rg_conv_w[0],
                  rg_conv_b[0].reshape(1, D_RNN), rg_gate_w[0].astype(BF16), rg_gate_b[0],
                  rg_lam[0].reshape(1, D_RNN), rg_w_out[0].astype(BF16), *ln(0, 1))
    h = _ffn_ln(h.reshape(TOKENS, D_MODEL), *ffn(0, 1), *ln(0, 2))
    k1, k4, k16, v1, v4, v16 = _proj(h, kv_w.astype(BF16), [(0, DILATIONS), (1, DILATIONS)])

    h = _ffn_ln(h, *ffn(1, 0), *ln(1, 0))
    qs = _proj(h, attn_w_q[0].astype(BF16), [(g, (d,)) for g, d in enumerate(DILATIONS)],
               scale=1.0 / math.sqrt(HEAD_DIM))
    outs, lses = [], []
    for q_g, k_g, v_g, dilation in zip(qs, (k1, k4, k16), (v1, v4, v16), DILATIONS):
        o_g, l_g = _attention_group(q_g, k_g, v_g, dilation)
        outs.append(o_g)
        lses.append(l_g)
    h = _attn_out_ln(h, outs, lses, attn_w_o[0].astype(BF16), *ln(1, 1))
    h = _ffn_ln(h, *ffn(1, 1), *ln(1, 2))
    return h.reshape(BATCH, SEQ, D_MODEL)
```

```python
import functools
import math

import jax
import jax.numpy as jnp
from jax import lax
from jax.experimental import pallas as pl
from jax.experimental.pallas import tpu as pltpu

D_MODEL = 1024
BATCH = 8
SEQ = 4096
DEPTH = 2
D_FF = 2816
D_RNN = 1280
N_GATE_BLOCKS = 5
GATE_BW = D_RNN // N_GATE_BLOCKS
CONV_WIDTH = 4
LRU_C = 8.0
N_HEADS = 16
HEAD_DIM = D_MODEL // N_HEADS
DILATION_PAIRS = ((128, 1), (512, 4), (2048, 16))
DILATIONS = tuple(d for _, d in DILATION_PAIRS)
N_GROUPS = len(DILATION_PAIRS)
DEEPNORM_ALPHA = float((2 * DEPTH) ** 0.25)
LN_EPS = 1e-5
NEG_INF = -1e30

TOKENS = BATCH * SEQ
ROW_TILE = 512
TILES_PER_SEQ = SEQ // ROW_TILE
FFN_SUB_TILE = 512
FFN_SUB_TILES = 2
TIME_TILE = 64
RG_ROWS = TIME_TILE * BATCH
FF_CHUNK = 256
ATT_BLK = 128
LANES = 128
D_TILES = D_MODEL // LANES
PAIR = 2 * HEAD_DIM
PHASE_STEP = 4
STAGE_PITCH = TIME_TILE + 8
ATT_ROWS_PER_STEP = 512
STAT_SHIFT = 16
LOG2_E = math.log2(math.e)
VMEM_LIMIT = 56 * 1024 * 1024

assert all(w // d == ATT_BLK for w, d in DILATION_PAIRS)
assert DILATIONS == (1, PHASE_STEP, PHASE_STEP * PHASE_STEP)
assert PAIR == LANES

BF16 = jnp.bfloat16
F32 = jnp.float32


def _dot(a, b):
    return jnp.dot(a, b, preferred_element_type=F32)


def _dot_nt(a, b):
    return lax.dot_general(a, b, (((1,), (1,)), ((), ())), preferred_element_type=F32)


def _sigmoid(x):
    return 0.5 * jnp.tanh(0.5 * x) + 0.5


def _gelu_tanh(x):
    c = math.sqrt(2.0 / math.pi)
    return 0.5 * x * (1.0 + jnp.tanh(c * (x + 0.044715 * (x * x * x))))


def _layer_norm(z, g, b):
    mu = jnp.mean(z, axis=-1, keepdims=True)
    zc = z - mu
    var = jnp.mean(zc * zc, axis=-1, keepdims=True)
    return zc * lax.rsqrt(var + LN_EPS) * g + b


def _lane_tile(c):
    return slice(c * LANES, (c + 1) * LANES)


def _store_lane_tiles(tile_ref, value):
    for c in range(tile_ref.shape[0]):
        tile_ref[c] = value[:, _lane_tile(c)]


def _resident(shape):
    nd = len(shape)
    return pl.BlockSpec(shape, lambda *_: (0,) * nd, pipeline_mode=pl.Buffered(1))


def _resident_slice(shape, lead):
    n_tail = len(shape) - len(lead)
    block = (None,) * len(lead) + tuple(shape[len(lead):])
    return pl.BlockSpec(block, lambda *_: tuple(lead) + (0,) * n_tail, pipeline_mode=pl.Buffered(1))


def _row_spec(width, rows=ROW_TILE):
    return pl.BlockSpec((rows, width), lambda i: (i, 0))


def _phase_spec(dilation, width):
    return pl.BlockSpec((None, dilation, ROW_TILE // dilation, width),
                        lambda i: (i // TILES_PER_SEQ, 0, i % TILES_PER_SEQ, 0))


def _params(n_grid):
    return pltpu.CompilerParams(dimension_semantics=("arbitrary",) * n_grid,
                                vmem_limit_bytes=VMEM_LIMIT)


def _ffn_kernel(x_ref, w_in_ref, w_out_ref, g_ref, b_ref, o_ref, act_ref):
    for t in range(FFN_SUB_TILES):
        rows = slice(t * FFN_SUB_TILE, (t + 1) * FFN_SUB_TILE)
        x = x_ref[rows, :]
        xb = x.astype(BF16)
        for c in range(D_FF // FF_CHUNK):
            lo = c * FF_CHUNK
            gate = _dot(xb, w_in_ref[:, lo:lo + FF_CHUNK])
            up = _dot(xb, w_in_ref[:, D_FF + lo:D_FF + lo + FF_CHUNK])
            act_ref[t, :, lo:lo + FF_CHUNK] = (gate * _sigmoid(gate) * up).astype(BF16)
        y = _dot(act_ref[t], w_out_ref[...])
        z = DEEPNORM_ALPHA * x + 0.5 * y
        o_ref[rows, :] = _layer_norm(z, g_ref[...], b_ref[...])


def _ffn_ln(x, w_in, w_out, g, b, layer, k, ln_k):
    rows = FFN_SUB_TILE * FFN_SUB_TILES
    return pl.pallas_call(
        _ffn_kernel,
        grid=(TOKENS // rows,),
        in_specs=[_row_spec(D_MODEL, rows), _resident_slice(w_in.shape, (layer, k)),
                  _resident_slice(w_out.shape, (layer, k)),
                  _resident_slice(g.shape, (layer, ln_k)), _resident_slice(b.shape, (layer, ln_k))],
        out_specs=_row_spec(D_MODEL, rows),
        out_shape=jax.ShapeDtypeStruct((TOKENS, D_MODEL), F32),
        scratch_shapes=[pltpu.VMEM((FFN_SUB_TILES, FFN_SUB_TILE, D_FF), BF16)],
        compiler_params=_params(1),
        name="ffn_ln",
    )(x, w_in, w_out, g, b)


def _rglru_kernel(x_ref, w_in_ref, conv_w_ref, conv_b_ref, gate_w_ref, gate_b_ref, lam_ref,
                  w_out_ref, g_ref, b_ref, o_ref, stage_in_ref, stage_out_ref, xs_ref, ubuf_ref,
                  a_ref, h_ref, carry_ref):
    tm = RG_ROWS
    halo = (CONV_WIDTH - 1) * BATCH

    @pl.when(pl.program_id(0) == 0)
    def _():
        ubuf_ref[0:halo, :] = jnp.zeros((halo, D_RNN), F32)
        carry_ref[...] = jnp.zeros((BATCH, D_RNN), F32)

    for bb in range(BATCH):
        for c in range(D_TILES):
            stage_in_ref[c, bb * STAGE_PITCH:bb * STAGE_PITCH + TIME_TILE, :] = x_ref[bb, :, _lane_tile(c)]
    for t in range(TIME_TILE):
        for c in range(D_TILES):
            xs_ref[t * BATCH:(t + 1) * BATCH, _lane_tile(c)] = (
                stage_in_ref[c, pl.ds(t, BATCH, stride=STAGE_PITCH), :])

    x = xs_ref[...]
    xb = x.astype(BF16)
    y_br = _dot(xb, w_in_ref[:, 0:D_RNN])
    u_br = _dot(xb, w_in_ref[:, D_RNN:2 * D_RNN])

    ubuf_ref[halo:halo + tm, :] = u_br
    u = conv_b_ref[...] + conv_w_ref[CONV_WIDTH - 1:CONV_WIDTH, :] * u_br
    for j in range(CONV_WIDTH - 1):
        u = u + conv_w_ref[j:j + 1, :] * ubuf_ref[j * BATCH:j * BATCH + tm, :]
    ubuf_ref[0:halo, :] = u_br[tm - halo:tm, :]

    ub = u.astype(BF16)
    rec_parts, in_parts = [], []
    for n in range(N_GATE_BLOCKS):
        blk = ub[:, n * GATE_BW:(n + 1) * GATE_BW]
        rec_parts.append(_dot(blk, gate_w_ref[0, n]))
        in_parts.append(_dot(blk, gate_w_ref[1, n]))
    rec_gate = _sigmoid(jnp.concatenate(rec_parts, axis=1) + gate_b_ref[0:1, :])
    in_gate = _sigmoid(jnp.concatenate(in_parts, axis=1) + gate_b_ref[1:2, :])

    neg_lam = -lam_ref[...]
    softplus = jnp.maximum(neg_lam, 0.0) + jnp.log1p(jnp.exp(-jnp.abs(neg_lam)))
    log_a = (-LRU_C) * rec_gate * softplus
    a = jnp.exp(log_a)
    one_m_a2 = 1.0 - a * a
    a_ref[...] = a
    h_ref[...] = (one_m_a2 * lax.rsqrt(jnp.maximum(one_m_a2, 1e-30))) * (in_gate * u)

    def step(t, h):
        r = pl.multiple_of(t * BATCH, BATCH)
        h = a_ref[pl.ds(r, BATCH), :] * h + h_ref[pl.ds(r, BATCH), :]
        h_ref[pl.ds(r, BATCH), :] = h
        return h

    carry_ref[...] = lax.fori_loop(0, TIME_TILE, step, carry_ref[...], unroll=8)

    mixed = (_gelu_tanh(y_br) * h_ref[...]).astype(BF16)
    z = DEEPNORM_ALPHA * x + _dot(mixed, w_out_ref[...])
    _store_lane_tiles(stage_out_ref, _layer_norm(z, g_ref[...], b_ref[...]))
    for bb in range(BATCH):
        for c in range(D_TILES):
            o_ref[bb, :, _lane_tile(c)] = stage_out_ref[c, pl.ds(bb, TIME_TILE, stride=BATCH), :]


def _rglru_ln(x, w_in, conv_w, conv_b, gate_w, gate_b, lam, w_out, g, b):
    tm = RG_ROWS
    halo = (CONV_WIDTH - 1) * BATCH
    blk = pl.BlockSpec((BATCH, TIME_TILE, D_MODEL), lambda i: (0, i, 0))
    return pl.pallas_call(
        _rglru_kernel,
        grid=(SEQ // TIME_TILE,),
        in_specs=[blk,
                  _resident((D_MODEL, 2 * D_RNN)), _resident((CONV_WIDTH, D_RNN)),
                  _resident((1, D_RNN)), _resident((2, N_GATE_BLOCKS, GATE_BW, GATE_BW)),
                  _resident((2, D_RNN)), _resident((1, D_RNN)), _resident((D_RNN, D_MODEL)),
                  _resident((1, D_MODEL)), _resident((1, D_MODEL))],
        out_specs=blk,
        out_shape=jax.ShapeDtypeStruct((BATCH, SEQ, D_MODEL), F32),
        scratch_shapes=[pltpu.VMEM((D_TILES, BATCH * STAGE_PITCH, LANES), F32),
                        pltpu.VMEM((D_TILES, tm, LANES), F32), pltpu.VMEM((tm, D_MODEL), F32),
                        pltpu.VMEM((tm + halo, D_RNN), F32), pltpu.VMEM((tm, D_RNN), F32),
                        pltpu.VMEM((tm, D_RNN), F32), pltpu.VMEM((BATCH, D_RNN), F32)],
        compiler_params=_params(1),
        name="rglru_ln",
    )(x, w_in, conv_w, conv_b, gate_w, gate_b, lam, w_out, g, b)


def _proj_kernel(x_ref, w_ref, *refs, plan, scale):
    n_out = sum(len(dils) for _, dils in plan)
    o_refs, scratch = list(refs[:n_out]), refs[n_out:]
    xb = x_ref[...].astype(BF16)
    quarter = ROW_TILE // PHASE_STEP
    for k, (cb, dils) in enumerate(plan):
        y = _dot(xb, w_ref[:, cb * D_MODEL:(cb + 1) * D_MODEL])
        if scale != 1.0:
            y = y * scale
        outs = {d: o_refs.pop(0) for d in dils}
        nat_ref, ph_ref = scratch[2 * k], scratch[2 * k + 1]
        if 1 in outs:
            outs[1][0] = y.astype(BF16)
        if all(d == 1 for d in outs):
            continue
        _store_lane_tiles(nat_ref, y)
        for c in range(D_TILES):
            for p in range(PHASE_STEP):
                v = nat_ref[c, pl.ds(p, quarter, stride=PHASE_STEP), :]
                if PHASE_STEP in outs:
                    outs[PHASE_STEP][p, :, _lane_tile(c)] = v.astype(BF16)
                if PHASE_STEP ** 2 in outs:
                    ph_ref[c, p * quarter:(p + 1) * quarter, :] = v
        if PHASE_STEP ** 2 in outs:
            sub = quarter // PHASE_STEP
            for c in range(D_TILES):
                for p in range(PHASE_STEP):
                    for j in range(PHASE_STEP):
                        v = ph_ref[c, pl.ds(p * quarter + j, sub, stride=PHASE_STEP), :]
                        outs[PHASE_STEP ** 2][p + PHASE_STEP * j, :, _lane_tile(c)] = v.astype(BF16)


def _proj(x, w, plan, scale=1.0):
    dil_list = [d for _, dils in plan for d in dils]
    tile_scratch = pltpu.VMEM((D_TILES, ROW_TILE, LANES), F32)
    return pl.pallas_call(
        functools.partial(_proj_kernel, plan=plan, scale=scale),
        grid=(TOKENS // ROW_TILE,),
        in_specs=[_row_spec(D_MODEL), _resident((D_MODEL, w.shape[1]))],
        out_specs=[_phase_spec(d, D_MODEL) for d in dil_list],
        out_shape=[jax.ShapeDtypeStruct((BATCH, d, SEQ // d, D_MODEL), BF16) for d in dil_list],
        scratch_shapes=[tile_scratch] * (2 * len(plan)),
        compiler_params=_params(1),
        name=f"proj{len(dil_list)}",
    )(x, w)


def _attn_kernel(q_ref, kp_ref, kc_ref, vp_ref, vc_ref, o_ref, stat_ref, bias_ref, *, dilation, phases, qrows):
    is_first_step = ((pl.program_id(0) == 0) & (pl.program_id(1) == 0) & (pl.program_id(2) == 0))

    @pl.when(is_first_step)
    def _():
        row = lax.broadcasted_iota(jnp.int32, (ATT_BLK, 2 * ATT_BLK), 0)
        col = lax.broadcasted_iota(jnp.int32, (ATT_BLK, 2 * ATT_BLK), 1)
        dist = row + ATT_BLK - col
        valid = (dist >= 0) & (dist <= ATT_BLK)
        dist_f = dist.astype(F32)
        for head in range(N_HEADS):
            slope = 2.0 ** (-8.0 * (head + 1) / N_HEADS)
            bias = dist_f * (-slope * dilation * LOG2_E)
            bias_ref[head] = jnp.where(valid, bias, NEG_INF)
            bias_ref[N_HEADS + head] = jnp.where(valid & (col >= ATT_BLK), bias, NEG_INF)

    lane = lax.broadcasted_iota(jnp.int32, (1, LANES), 1)
    lane_lo = lane < HEAD_DIM
    zero = jnp.zeros((), BF16)
    ones = jnp.ones((2 * ATT_BLK, LANES), BF16)
    first_base = jnp.where(pl.program_id(2) == 0, N_HEADS, 0)

    for ph in range(phases):
        for j in range(qrows // ATT_BLK):
            r0 = j * ATT_BLK
            base = first_base if j == 0 else 0
            stat_tile = jnp.zeros((ATT_BLK, LANES), F32)
            for pair in range(N_HEADS // 2):
                lanes = slice(pair * PAIR, (pair + 1) * PAIR)
                q2 = q_ref[ph, r0:r0 + ATT_BLK, lanes]
                if j == 0:
                    kw = jnp.concatenate([kp_ref[ph, :, lanes], kc_ref[ph, 0:ATT_BLK, lanes]], axis=0)
                    vw = jnp.concatenate([vp_ref[ph, :, lanes], vc_ref[ph, 0:ATT_BLK, lanes]], axis=0)
                else:
                    kw = kc_ref[ph, r0 - ATT_BLK:r0 + ATT_BLK, lanes]
                    vw = vc_ref[ph, r0 - ATT_BLK:r0 + ATT_BLK, lanes]
                q_st = jnp.concatenate([jnp.where(lane_lo, q2, zero), jnp.where(lane_lo, zero, q2)], axis=0)
                bias = jnp.concatenate([bias_ref[base + 2 * pair], bias_ref[base + 2 * pair + 1]], axis=0)
                s = _dot_nt(q_st, kw) + bias
                m = jnp.max(s, axis=-1, keepdims=True)
                acc = _dot(jnp.exp2(s - m).astype(BF16), jnp.concatenate([vw, ones], axis=1))
                num = jnp.where(lane_lo, acc[:ATT_BLK, :LANES], acc[ATT_BLK:, :LANES])
                den = jnp.where(lane_lo, acc[:ATT_BLK, LANES:], acc[ATT_BLK:, LANES:])
                o_ref[ph, r0:r0 + ATT_BLK, lanes] = (num * (1.0 / den)).astype(BF16)
                slot = lane % HEAD_DIM
                stat_tile = jnp.where(slot == pair, jnp.where(lane_lo, m[:ATT_BLK], m[ATT_BLK:]), stat_tile)
                stat_tile = jnp.where(slot == STAT_SHIFT + pair, den, stat_tile)
            stat_ref[ph, r0:r0 + ATT_BLK, :] = stat_tile


def _attention_group(q, k, v, dilation):
    s_d = SEQ // dilation
    qrows = min(ATT_ROWS_PER_STEP, s_d)
    phases = ATT_ROWS_PER_STEP // qrows
    ratio = qrows // ATT_BLK
    cur = pl.BlockSpec((None, phases, qrows, D_MODEL), lambda bb, p, t: (bb, p, t, 0))
    prev = pl.BlockSpec((None, phases, ATT_BLK, D_MODEL),
                        lambda bb, p, t: (bb, p, jnp.maximum(t * ratio - 1, 0), 0))
    return pl.pallas_call(
        functools.partial(_attn_kernel, dilation=dilation, phases=phases, qrows=qrows),
        grid=(BATCH, dilation // phases, s_d // qrows),
        in_specs=[cur, prev, cur, prev, cur],
        out_specs=[cur, pl.BlockSpec((None, phases, qrows, LANES), lambda bb, p, t: (bb, p, t, 0))],
        out_shape=[jax.ShapeDtypeStruct((BATCH, dilation, s_d, D_MODEL), BF16),
                   jax.ShapeDtypeStruct((BATCH, dilation, s_d, LANES), F32)],
        scratch_shapes=[pltpu.VMEM((2 * N_HEADS, ATT_BLK, 2 * ATT_BLK), F32)],
        compiler_params=_params(3),
        name=f"attn_d{dilation}",
    )(q, k, k, v, v)


def _to_natural(src_ref, dilation, nat_ref, ph_ref):
    if dilation == 1:
        return src_ref[0].astype(F32)
    n_tiles = src_ref.shape[-1] // LANES
    quarter = ROW_TILE // PHASE_STEP
    for c in range(n_tiles):
        for p in range(PHASE_STEP):
            if dilation == PHASE_STEP:
                v = src_ref[p, :, _lane_tile(c)].astype(F32)
            else:
                sub = quarter // PHASE_STEP
                for j in range(PHASE_STEP):
                    ph_ref[c, pl.ds(p * quarter + j, sub, stride=PHASE_STEP), :] = (
                        src_ref[p + PHASE_STEP * j, :, _lane_tile(c)].astype(F32))
                v = ph_ref[c, p * quarter:(p + 1) * quarter, :]
            nat_ref[c, pl.ds(p, quarter, stride=PHASE_STEP), :] = v
    return jnp.concatenate([nat_ref[c] for c in range(n_tiles)], axis=1)


def _attn_out_kernel(x_ref, o0_ref, o1_ref, o2_ref, l0_ref, l1_ref, l2_ref, expand_ref, w_o_ref,
                     g_ref, b_ref, out_ref, *scratch):
    o_refs, l_refs = (o0_ref, o1_ref, o2_ref), (l0_ref, l1_ref, l2_ref)
    stats = [_to_natural(l_refs[g], d, scratch[4 * g + 2], scratch[4 * g + 3])
             for g, d in enumerate(DILATIONS)]
    dens = [pltpu.roll(st, LANES - STAT_SHIFT, axis=1) for st in stats]
    m = jnp.maximum(jnp.maximum(stats[0], stats[1]), stats[2])
    es = [jnp.exp2(st - m) * den for st, den in zip(stats, dens)]
    inv = 1.0 / (es[0] + es[1] + es[2])
    lane = lax.broadcasted_iota(jnp.int32, (1, LANES), 1)
    is_head_slot = (lane % HEAD_DIM) < N_HEADS // 2
    comb = None
    for g, dilation in enumerate(DILATIONS):
        w = jnp.where(is_head_slot, es[g] * inv, 0.0)
        w_hi = w.astype(BF16)
        w_lo = (w - w_hi.astype(F32)).astype(BF16)
        w_full = _dot(w_hi, expand_ref[...]) + _dot(w_lo, expand_ref[...])
        term = w_full * _to_natural(o_refs[g], dilation, scratch[4 * g], scratch[4 * g + 1])
        comb = term if comb is None else comb + term
    mix = _dot(comb.astype(BF16), w_o_ref[...])
    z = DEEPNORM_ALPHA * x_ref[...] + mix
    out_ref[...] = _layer_norm(z, g_ref[...], b_ref[...])


def _attn_out_ln(x, outs, lses, w_o, g, b):
    lane = jnp.arange(LANES, dtype=jnp.int32)
    head_of_lane = jnp.where(lane < HEAD_DIM, 2 * lane, 2 * (lane - HEAD_DIM) + 1)
    used = (lane % HEAD_DIM) < N_HEADS // 2
    head_of_col = jnp.arange(D_MODEL, dtype=jnp.int32) // HEAD_DIM
    expand = ((head_of_lane[:, None] == head_of_col[None, :]) & used[:, None]).astype(BF16)
    wide = pltpu.VMEM((D_TILES, ROW_TILE, LANES), F32)
    narrow = pltpu.VMEM((1, ROW_TILE, LANES), F32)
    return pl.pallas_call(
        _attn_out_kernel,
        grid=(TOKENS // ROW_TILE,),
        in_specs=[_row_spec(D_MODEL)] + [_phase_spec(d, D_MODEL) for d in DILATIONS] +
                 [_phase_spec(d, LANES) for d in DILATIONS] +
                 [_resident((LANES, D_MODEL)), _resident((D_MODEL, D_MODEL)),
                  _resident((1, D_MODEL)), _resident((1, D_MODEL))],
        out_specs=_row_spec(D_MODEL),
        out_shape=jax.ShapeDtypeStruct((TOKENS, D_MODEL), F32),
        scratch_shapes=[wide, wide, narrow, narrow] * N_GROUPS,
        compiler_params=_params(1),
        name="attn_out_ln",
    )(x, *outs, *lses, expand, w_o, g, b)


def kernel(x, ln_g, ln_b, ffn_w_in, ffn_w_out, rg_w_in, rg_conv_w, rg_conv_b, rg_gate_w, rg_gate_b,
           rg_lam, rg_w_out, kv_w, attn_w_q, attn_w_o):
    assert x.shape == (BATCH, SEQ, D_MODEL)
    ln = lambda layer, k: (ln_g[layer, k].reshape(1, D_MODEL), ln_b[layer, k].reshape(1, D_MODEL))
    g4, b4 = ln_g.reshape(DEPTH, 3, 1, D_MODEL), ln_b.reshape(DEPTH, 3, 1, D_MODEL)
    w_in, w_out = ffn_w_in.astype(BF16), ffn_w_out.astype(BF16)
    ffn = lambda h, layer, k: _ffn_ln(h, w_in, w_out, g4, b4, layer, k, 2 * k)

    h = ffn(x.reshape(TOKENS, D_MODEL), 0, 0)
    h = _rglru_ln(h.reshape(BATCH, SEQ, D_MODEL), rg_w_in[0].astype(BF16), rg_conv_w[0],
                  rg_conv_b[0].reshape(1, D_RNN), rg_gate_w[0].astype(BF16), rg_gate_b[0],
                  rg_lam[0].reshape(1, D_RNN), rg_w_out[0].astype(BF16), *ln(0, 1))
    h = ffn(h.reshape(TOKENS, D_MODEL), 0, 1)
    k1, k4, k16, v1, v4, v16 = _proj(h, kv_w.astype(BF16), [(0, DILATIONS), (1, DILATIONS)])

    h = ffn(h, 1, 0)
    qs = _proj(h, attn_w_q[0].astype(BF16), [(g, (d,)) for g, d in enumerate(DILATIONS)],
               scale=LOG2_E / math.sqrt(HEAD_DIM))
    outs, lses = [], []
    for q_g, k_g, v_g, dilation in zip(qs, (k1, k4, k16), (v1, v4, v16), DILATIONS):
        o_g, l_g = _attention_group(q_g, k_g, v_g, dilation)
        outs.append(o_g)
        lses.append(l_g)
    h = _attn_out_ln(h, outs, lses, attn_w_o[0].astype(BF16), *ln(1, 1))
    h = ffn(h, 1, 1)
    return h.reshape(BATCH, SEQ, D_MODEL)
```

```python
import functools
import math

import jax
import jax.numpy as jnp
from jax import lax
from jax.experimental import pallas as pl
from jax.experimental.pallas import tpu as pltpu

D_MODEL = 1024
BATCH = 8
SEQ = 4096
DEPTH = 2
D_FF = 2816
D_RNN = 1280
N_GATE_BLOCKS = 5
GATE_BW = D_RNN // N_GATE_BLOCKS
CONV_WIDTH = 4
LRU_C = 8.0
N_HEADS = 16
HEAD_DIM = D_MODEL // N_HEADS
DILATION_PAIRS = ((128, 1), (512, 4), (2048, 16))
DILATIONS = tuple(d for _, d in DILATION_PAIRS)
N_GROUPS = len(DILATION_PAIRS)
DEEPNORM_ALPHA = float((2 * DEPTH) ** 0.25)
LN_EPS = 1e-5
NEG_INF = -1e30

TOKENS = BATCH * SEQ
ROW_TILE = 512
TILES_PER_SEQ = SEQ // ROW_TILE
FFN_SUB_TILE = 512
FFN_SUB_TILES = 2
TIME_TILE = 64
RG_ROWS = TIME_TILE * BATCH
FF_CHUNK = 256
ATT_BLK = 128
LANES = 128
D_TILES = D_MODEL // LANES
PAIR = 2 * HEAD_DIM
PHASE_STEP = 4
STAGE_PITCH = TIME_TILE + 8
ATT_ROWS_PER_STEP = 512
STAT_SHIFT = 16
LOG2_E = math.log2(math.e)
VMEM_LIMIT = 56 * 1024 * 1024

assert all(w // d == ATT_BLK for w, d in DILATION_PAIRS)
assert DILATIONS == (1, PHASE_STEP, PHASE_STEP * PHASE_STEP)
assert PAIR == LANES

BF16 = jnp.bfloat16
F32 = jnp.float32


def _dot(a, b):
    return jnp.dot(a, b, preferred_element_type=F32)


def _dot_nt(a, b):
    return lax.dot_general(a, b, (((1,), (1,)), ((), ())), preferred_element_type=F32)


def _sigmoid(x):
    return 0.5 * jnp.tanh(0.5 * x) + 0.5


def _gelu_tanh(x):
    c = math.sqrt(2.0 / math.pi)
    return 0.5 * x * (1.0 + jnp.tanh(c * (x + 0.044715 * (x * x * x))))


def _layer_norm(z, g, b):
    mu = jnp.mean(z, axis=-1, keepdims=True)
    zc = z - mu
    var = jnp.mean(zc * zc, axis=-1, keepdims=True)
    return zc * lax.rsqrt(var + LN_EPS) * g + b


def _lane_tile(c):
    return slice(c * LANES, (c + 1) * LANES)


def _store_lane_tiles(tile_ref, value):
    for c in range(tile_ref.shape[0]):
        tile_ref[c] = value[:, _lane_tile(c)]


def _resident(shape):
    nd = len(shape)
    return pl.BlockSpec(shape, lambda *_: (0,) * nd, pipeline_mode=pl.Buffered(1))


def _resident_slice(shape, lead):
    n_tail = len(shape) - len(lead)
    block = (None,) * len(lead) + tuple(shape[len(lead):])
    return pl.BlockSpec(block, lambda *_: tuple(lead) + (0,) * n_tail, pipeline_mode=pl.Buffered(1))


def _row_spec(width, rows=ROW_TILE):
    return pl.BlockSpec((rows, width), lambda i: (i, 0))


def _phase_spec(dilation, width):
    return pl.BlockSpec((None, dilation, ROW_TILE // dilation, width),
                        lambda i: (i // TILES_PER_SEQ, 0, i % TILES_PER_SEQ, 0))


def _params(n_grid):
    return pltpu.CompilerParams(dimension_semantics=("arbitrary",) * n_grid,
                                vmem_limit_bytes=VMEM_LIMIT)


def _ffn_kernel(x_ref, w_in_ref, w_out_ref, g_ref, b_ref, o_ref, act_ref):
    for t in range(FFN_SUB_TILES):
        rows = slice(t * FFN_SUB_TILE, (t + 1) * FFN_SUB_TILE)
        xb = x_ref[rows, :].astype(BF16)
        for c in range(D_FF // FF_CHUNK):
            lo = c * FF_CHUNK
            gate = _dot(xb, w_in_ref[:, lo:lo + FF_CHUNK])
            up = _dot(xb, w_in_ref[:, D_FF + lo:D_FF + lo + FF_CHUNK])
            act_ref[t, :, lo:lo + FF_CHUNK] = (gate * _sigmoid(gate) * up).astype(BF16)
    for t in range(FFN_SUB_TILES):
        rows = slice(t * FFN_SUB_TILE, (t + 1) * FFN_SUB_TILE)
        y = _dot(act_ref[t], w_out_ref[...])
        z = DEEPNORM_ALPHA * x_ref[rows, :] + 0.5 * y
        o_ref[rows, :] = _layer_norm(z, g_ref[...], b_ref[...])


def _ffn_ln(x, w_in, w_out, g, b, layer, k, ln_k):
    rows = FFN_SUB_TILE * FFN_SUB_TILES
    return pl.pallas_call(
        _ffn_kernel,
        grid=(TOKENS // rows,),
        in_specs=[_row_spec(D_MODEL, rows), _resident_slice(w_in.shape, (layer, k)),
                  _resident_slice(w_out.shape, (layer, k)),
                  _resident_slice(g.shape, (layer, ln_k)), _resident_slice(b.shape, (layer, ln_k))],
        out_specs=_row_spec(D_MODEL, rows),
        out_shape=jax.ShapeDtypeStruct((TOKENS, D_MODEL), F32),
        scratch_shapes=[pltpu.VMEM((FFN_SUB_TILES, FFN_SUB_TILE, D_FF), BF16)],
        compiler_params=_params(1),
        name="ffn_ln",
    )(x, w_in, w_out, g, b)


def _rglru_kernel(x_ref, w_in_ref, conv_w_ref, conv_b_ref, gate_w_ref, gate_b_ref, lam_ref,
                  w_out_ref, g_ref, b_ref, o_ref, stage_in_ref, stage_out_ref, xs_ref, y_ref, mixed_ref,
                  ubuf_ref, a_ref, h_ref, carry_ref):
    step_id = pl.program_id(0)
    halo = (CONV_WIDTH - 1) * BATCH

    @pl.when(step_id == 0)
    def _():
        ubuf_ref[0:halo, :] = jnp.zeros((halo, D_RNN), F32)
        carry_ref[...] = jnp.zeros((BATCH, D_RNN), F32)
        h_ref[...] = jnp.zeros(h_ref.shape, F32)
        y_ref[1] = jnp.zeros(y_ref.shape[1:], F32)
        xs_ref[1] = jnp.zeros(xs_ref.shape[1:], F32)

    for parity in range(2):
        @pl.when(step_id % 2 == parity)
        def _(cur=parity, prev=1 - parity):
            _rglru_step(x_ref, w_in_ref, conv_w_ref, conv_b_ref, gate_w_ref, gate_b_ref, lam_ref,
                        w_out_ref, g_ref, b_ref, o_ref, stage_in_ref, stage_out_ref, xs_ref.at[cur],
                        xs_ref.at[prev], y_ref.at[cur], y_ref.at[prev], mixed_ref, ubuf_ref, a_ref, h_ref,
                        carry_ref)


def _rglru_step(x_ref, w_in_ref, conv_w_ref, conv_b_ref, gate_w_ref, gate_b_ref, lam_ref,
                w_out_ref, g_ref, b_ref, o_ref, stage_in_ref, stage_out_ref, xs_ref, xs_prev_ref,
                y_ref, y_prev_ref, mixed_ref, ubuf_ref, a_ref, h_ref, carry_ref):
    tm = RG_ROWS
    halo = (CONV_WIDTH - 1) * BATCH

    for bb in range(BATCH):
        for c in range(D_TILES):
            stage_in_ref[c, bb * STAGE_PITCH:bb * STAGE_PITCH + TIME_TILE, :] = x_ref[bb, :, _lane_tile(c)]
    for t in range(TIME_TILE):
        for c in range(D_TILES):
            xs_ref[t * BATCH:(t + 1) * BATCH, _lane_tile(c)] = (
                stage_in_ref[c, pl.ds(t, BATCH, stride=STAGE_PITCH), :])

    xb = xs_ref[...].astype(BF16)
    z = DEEPNORM_ALPHA * xs_prev_ref[...]
    for n in range(N_GATE_BLOCKS):
        cols = slice(n * GATE_BW, (n + 1) * GATE_BW)
        mixed = (_gelu_tanh(y_prev_ref[:, cols]) * h_ref[:, cols]).astype(BF16)
        y_ref[:, cols] = _dot(xb, w_in_ref[:, n * GATE_BW:(n + 1) * GATE_BW])
        ubuf_ref[halo:halo + tm, cols] = _dot(xb, w_in_ref[:, D_RNN + n * GATE_BW:D_RNN + (n + 1) * GATE_BW])
        z = z + _dot(mixed, w_out_ref[cols, :])

    neg_lam = -lam_ref[...]
    softplus = jnp.maximum(neg_lam, 0.0) + jnp.log1p(jnp.exp(-jnp.abs(neg_lam)))
    for n in range(N_GATE_BLOCKS):
        cols = slice(n * GATE_BW, (n + 1) * GATE_BW)
        u = conv_b_ref[:, cols] + conv_w_ref[CONV_WIDTH - 1:CONV_WIDTH, cols] * ubuf_ref[halo:halo + tm, cols]
        for j in range(CONV_WIDTH - 1):
            u = u + conv_w_ref[j:j + 1, cols] * ubuf_ref[j * BATCH:j * BATCH + tm, cols]
        ubuf_ref[0:halo, cols] = ubuf_ref[tm:tm + halo, cols]
        ub = u.astype(BF16)
        rec_gate = _sigmoid(_dot(ub, gate_w_ref[0, n]) + gate_b_ref[0:1, cols])
        in_gate = _sigmoid(_dot(ub, gate_w_ref[1, n]) + gate_b_ref[1:2, cols])
        a = jnp.exp((-LRU_C) * rec_gate * softplus[:, cols])
        one_m_a2 = 1.0 - a * a
        a_ref[:, cols] = a
        h_ref[:, cols] = (one_m_a2 * lax.rsqrt(jnp.maximum(one_m_a2, 1e-30))) * (in_gate * u)

    h = carry_ref[...]
    for t in range(TIME_TILE):
        rows = slice(t * BATCH, (t + 1) * BATCH)
        h = a_ref[rows, :] * h + h_ref[rows, :]
        h_ref[rows, :] = h
    carry_ref[...] = h

    _store_lane_tiles(stage_out_ref, _layer_norm(z, g_ref[...], b_ref[...]))
    for bb in range(BATCH):
        for c in range(D_TILES):
            o_ref[bb, :, _lane_tile(c)] = stage_out_ref[c, pl.ds(bb, TIME_TILE, stride=BATCH), :]


def _rglru_ln(x, w_in, conv_w, conv_b, gate_w, gate_b, lam, w_out, g, b):
    tm = RG_ROWS
    halo = (CONV_WIDTH - 1) * BATCH
    n_tiles = SEQ // TIME_TILE
    blk = pl.BlockSpec((BATCH, TIME_TILE, D_MODEL), lambda i: (0, jnp.minimum(i, n_tiles - 1), 0))
    out_blk = pl.BlockSpec((BATCH, TIME_TILE, D_MODEL), lambda i: (0, jnp.maximum(i - 1, 0), 0))
    return pl.pallas_call(
        _rglru_kernel,
        grid=(n_tiles + 1,),
        in_specs=[blk,
                  _resident((D_MODEL, 2 * D_RNN)), _resident((CONV_WIDTH, D_RNN)),
                  _resident((1, D_RNN)), _resident((2, N_GATE_BLOCKS, GATE_BW, GATE_BW)),
                  _resident((2, D_RNN)), _resident((1, D_RNN)), _resident((D_RNN, D_MODEL)),
                  _resident((1, D_MODEL)), _resident((1, D_MODEL))],
        out_specs=out_blk,
        out_shape=jax.ShapeDtypeStruct((BATCH, SEQ, D_MODEL), F32),
        scratch_shapes=[pltpu.VMEM((D_TILES, BATCH * STAGE_PITCH, LANES), F32),
                        pltpu.VMEM((D_TILES, tm, LANES), F32), pltpu.VMEM((2, tm, D_MODEL), F32),
                        pltpu.VMEM((2, tm, D_RNN), F32), pltpu.VMEM((tm, D_RNN), BF16),
                        pltpu.VMEM((tm + halo, D_RNN), F32), pltpu.VMEM((tm, D_RNN), F32),
                        pltpu.VMEM((tm, D_RNN), F32), pltpu.VMEM((BATCH, D_RNN), F32)],
        compiler_params=_params(1),
        name="rglru_ln",
    )(x, w_in, conv_w, conv_b, gate_w, gate_b, lam, w_out, g, b)


def _proj_kernel(x_ref, w_ref, *refs, plan, scale):
    n_out = sum(len(dils) for _, dils in plan)
    o_refs, scratch = list(refs[:n_out]), refs[n_out:]
    xb = x_ref[...].astype(BF16)
    quarter = ROW_TILE // PHASE_STEP
    for k, (cb, dils) in enumerate(plan):
        y = _dot(xb, w_ref[:, cb * D_MODEL:(cb + 1) * D_MODEL])
        if scale != 1.0:
            y = y * scale
        outs = {d: o_refs.pop(0) for d in dils}
        nat_ref, ph_ref = scratch[2 * k], scratch[2 * k + 1]
        if 1 in outs:
            outs[1][0] = y.astype(BF16)
        if all(d == 1 for d in outs):
            continue
        _store_lane_tiles(nat_ref, y)
        for c in range(D_TILES):
            for p in range(PHASE_STEP):
                v = nat_ref[c, pl.ds(p, quarter, stride=PHASE_STEP), :]
                if PHASE_STEP in outs:
                    outs[PHASE_STEP][p, :, _lane_tile(c)] = v.astype(BF16)
                if PHASE_STEP ** 2 in outs:
                    ph_ref[c, p * quarter:(p + 1) * quarter, :] = v
        if PHASE_STEP ** 2 in outs:
            sub = quarter // PHASE_STEP
            for c in range(D_TILES):
                for p in range(PHASE_STEP):
                    for j in range(PHASE_STEP):
                        v = ph_ref[c, pl.ds(p * quarter + j, sub, stride=PHASE_STEP), :]
                        outs[PHASE_STEP ** 2][p + PHASE_STEP * j, :, _lane_tile(c)] = v.astype(BF16)


def _proj(x, w, plan, scale=1.0):
    dil_list = [d for _, dils in plan for d in dils]
    tile_scratch = pltpu.VMEM((D_TILES, ROW_TILE, LANES), F32)
    return pl.pallas_call(
        functools.partial(_proj_kernel, plan=plan, scale=scale),
        grid=(TOKENS // ROW_TILE,),
        in_specs=[_row_spec(D_MODEL), _resident((D_MODEL, w.shape[1]))],
        out_specs=[_phase_spec(d, D_MODEL) for d in dil_list],
        out_shape=[jax.ShapeDtypeStruct((BATCH, d, SEQ // d, D_MODEL), BF16) for d in dil_list],
        scratch_shapes=[tile_scratch] * (2 * len(plan)),
        compiler_params=_params(1),
        name=f"proj{len(dil_list)}",
    )(x, w)


def _attn_kernel(q_ref, kp_ref, kc_ref, vp_ref, vc_ref, o_ref, stat_ref, bias_ref, *, dilation, phases, qrows):
    is_first_step = ((pl.program_id(0) == 0) & (pl.program_id(1) == 0) & (pl.program_id(2) == 0))

    @pl.when(is_first_step)
    def _():
        row = lax.broadcasted_iota(jnp.int32, (ATT_BLK, 2 * ATT_BLK), 0)
        col = lax.broadcasted_iota(jnp.int32, (ATT_BLK, 2 * ATT_BLK), 1)
        dist = row + ATT_BLK - col
        valid = (dist >= 0) & (dist <= ATT_BLK)
        dist_f = dist.astype(F32)
        for head in range(N_HEADS):
            slope = 2.0 ** (-8.0 * (head + 1) / N_HEADS)
            bias = dist_f * (-slope * dilation * LOG2_E)
            bias_ref[head] = jnp.where(valid, bias, NEG_INF)
            bias_ref[N_HEADS + head] = jnp.where(valid & (col >= ATT_BLK), bias, NEG_INF)

    lane = lax.broadcasted_iota(jnp.int32, (1, LANES), 1)
    lane_lo = lane < HEAD_DIM
    zero = jnp.zeros((), BF16)
    ones = jnp.ones((2 * ATT_BLK, LANES), BF16)
    first_base = jnp.where(pl.program_id(2) == 0, N_HEADS, 0)

    for ph in range(phases):
        for j in range(qrows // ATT_BLK):
            r0 = j * ATT_BLK
            base = first_base if j == 0 else 0
            stat_tile = jnp.zeros((ATT_BLK, LANES), F32)
            for pair in range(N_HEADS // 2):
                lanes = slice(pair * PAIR, (pair + 1) * PAIR)
                q2 = q_ref[ph, r0:r0 + ATT_BLK, lanes]
                if j == 0:
                    kw = jnp.concatenate([kp_ref[ph, :, lanes], kc_ref[ph, 0:ATT_BLK, lanes]], axis=0)
                    vw = jnp.concatenate([vp_ref[ph, :, lanes], vc_ref[ph, 0:ATT_BLK, lanes]], axis=0)
                else:
                    kw = kc_ref[ph, r0 - ATT_BLK:r0 + ATT_BLK, lanes]
                    vw = vc_ref[ph, r0 - ATT_BLK:r0 + ATT_BLK, lanes]
                q_st = jnp.concatenate([jnp.where(lane_lo, q2, zero), jnp.where(lane_lo, zero, q2)], axis=0)
                bias = jnp.concatenate([bias_ref[base + 2 * pair], bias_ref[base + 2 * pair + 1]], axis=0)
                s = _dot_nt(q_st, kw) + bias
                m = jnp.max(s, axis=-1, keepdims=True)
                acc = _dot(jnp.exp2(s - m).astype(BF16), jnp.concatenate([vw, ones], axis=1))
                num = jnp.where(lane_lo, acc[:ATT_BLK, :LANES], acc[ATT_BLK:, :LANES])
                den = jnp.where(lane_lo, acc[:ATT_BLK, LANES:], acc[ATT_BLK:, LANES:])
                o_ref[ph, r0:r0 + ATT_BLK, lanes] = (num * (1.0 / den)).astype(BF16)
                slot = lane % HEAD_DIM
                stat_tile = jnp.where(slot == pair, jnp.where(lane_lo, m[:ATT_BLK], m[ATT_BLK:]), stat_tile)
                stat_tile = jnp.where(slot == STAT_SHIFT + pair, den, stat_tile)
            stat_ref[ph, r0:r0 + ATT_BLK, :] = stat_tile


def _attention_group(q, k, v, dilation):
    s_d = SEQ // dilation
    qrows = min(ATT_ROWS_PER_STEP, s_d)
    phases = ATT_ROWS_PER_STEP // qrows
    ratio = qrows // ATT_BLK
    cur = pl.BlockSpec((None, phases, qrows, D_MODEL), lambda bb, p, t: (bb, p, t, 0))
    prev = pl.BlockSpec((None, phases, ATT_BLK, D_MODEL),
                        lambda bb, p, t: (bb, p, jnp.maximum(t * ratio - 1, 0), 0))
    return pl.pallas_call(
        functools.partial(_attn_kernel, dilation=dilation, phases=phases, qrows=qrows),
        grid=(BATCH, dilation // phases, s_d // qrows),
        in_specs=[cur, prev, cur, prev, cur],
        out_specs=[cur, pl.BlockSpec((None, phases, qrows, LANES), lambda bb, p, t: (bb, p, t, 0))],
        out_shape=[jax.ShapeDtypeStruct((BATCH, dilation, s_d, D_MODEL), BF16),
                   jax.ShapeDtypeStruct((BATCH, dilation, s_d, LANES), F32)],
        scratch_shapes=[pltpu.VMEM((2 * N_HEADS, ATT_BLK, 2 * ATT_BLK), F32)],
        compiler_params=_params(3),
        name=f"attn_d{dilation}",
    )(q, k, k, v, v)


def _to_natural(src_ref, dilation, nat_ref, ph_ref):
    if dilation == 1:
        return src_ref[0].astype(F32)
    n_tiles = src_ref.shape[-1] // LANES
    quarter = ROW_TILE // PHASE_STEP
    for c in range(n_tiles):
        for p in range(PHASE_STEP):
            if dilation == PHASE_STEP:
                v = src_ref[p, :, _lane_tile(c)].astype(F32)
            else:
                sub = quarter // PHASE_STEP
                for j in range(PHASE_STEP):
                    ph_ref[c, pl.ds(p * quarter + j, sub, stride=PHASE_STEP), :] = (
                        src_ref[p + PHASE_STEP * j, :, _lane_tile(c)].astype(F32))
                v = ph_ref[c, p * quarter:(p + 1) * quarter, :]
            nat_ref[c, pl.ds(p, quarter, stride=PHASE_STEP), :] = v
    return jnp.concatenate([nat_ref[c] for c in range(n_tiles)], axis=1)


def _attn_out_kernel(x_ref, o0_ref, o1_ref, o2_ref, l0_ref, l1_ref, l2_ref, expand_ref, w_o_ref,
                     g_ref, b_ref, out_ref, *scratch):
    o_refs, l_refs = (o0_ref, o1_ref, o2_ref), (l0_ref, l1_ref, l2_ref)
    stats = [_to_natural(l_refs[g], d, scratch[4 * g + 2], scratch[4 * g + 3])
             for g, d in enumerate(DILATIONS)]
    dens = [pltpu.roll(st, LANES - STAT_SHIFT, axis=1) for st in stats]
    m = jnp.maximum(jnp.maximum(stats[0], stats[1]), stats[2])
    es = [jnp.exp2(st - m) * den for st, den in zip(stats, dens)]
    inv = 1.0 / (es[0] + es[1] + es[2])
    lane = lax.broadcasted_iota(jnp.int32, (1, LANES), 1)
    is_head_slot = (lane % HEAD_DIM) < N_HEADS // 2
    comb = None
    for g, dilation in enumerate(DILATIONS):
        w = jnp.where(is_head_slot, es[g] * inv, 0.0)
        w_hi = w.astype(BF16)
        w_lo = (w - w_hi.astype(F32)).astype(BF16)
        w_full = _dot(jnp.concatenate([w_hi, w_lo], axis=1), expand_ref[...])
        term = w_full * _to_natural(o_refs[g], dilation, scratch[4 * g], scratch[4 * g + 1])
        comb = term if comb is None else comb + term
    mix = _dot(comb.astype(BF16), w_o_ref[...])
    z = DEEPNORM_ALPHA * x_ref[...] + mix
    out_ref[...] = _layer_norm(z, g_ref[...], b_ref[...])


def _attn_out_ln(x, outs, lses, w_o, g, b):
    lane = jnp.arange(LANES, dtype=jnp.int32)
    head_of_lane = jnp.where(lane < HEAD_DIM, 2 * lane, 2 * (lane - HEAD_DIM) + 1)
    used = (lane % HEAD_DIM) < N_HEADS // 2
    head_of_col = jnp.arange(D_MODEL, dtype=jnp.int32) // HEAD_DIM
    expand = ((head_of_lane[:, None] == head_of_col[None, :]) & used[:, None]).astype(BF16)
    expand = jnp.concatenate([expand, expand], axis=0)
    wide = pltpu.VMEM((D_TILES, ROW_TILE, LANES), F32)
    narrow = pltpu.VMEM((1, ROW_TILE, LANES), F32)
    return pl.pallas_call(
        _attn_out_kernel,
        grid=(TOKENS // ROW_TILE,),
        in_specs=[_row_spec(D_MODEL)] + [_phase_spec(d, D_MODEL) for d in DILATIONS] +
                 [_phase_spec(d, LANES) for d in DILATIONS] +
                 [_resident((2 * LANES, D_MODEL)), _resident((D_MODEL, D_MODEL)),
                  _resident((1, D_MODEL)), _resident((1, D_MODEL))],
        out_specs=_row_spec(D_MODEL),
        out_shape=jax.ShapeDtypeStruct((TOKENS, D_MODEL), F32),
        scratch_shapes=[wide, wide, narrow, narrow] * N_GROUPS,
        compiler_params=_params(1),
        name="attn_out_ln",
    )(x, *outs, *lses, expand, w_o, g, b)


def kernel(x, ln_g, ln_b, ffn_w_in, ffn_w_out, rg_w_in, rg_conv_w, rg_conv_b, rg_gate_w, rg_gate_b,
           rg_lam, rg_w_out, kv_w, attn_w_q, attn_w_o):
    assert x.shape == (BATCH, SEQ, D_MODEL)
    ln = lambda layer, k: (ln_g[layer, k].reshape(1, D_MODEL), ln_b[layer, k].reshape(1, D_MODEL))
    g4, b4 = ln_g.reshape(DEPTH, 3, 1, D_MODEL), ln_b.reshape(DEPTH, 3, 1, D_MODEL)
    w_in, w_out = ffn_w_in.astype(BF16), ffn_w_out.astype(BF16)
    ffn = lambda h, layer, k: _ffn_ln(h, w_in, w_out, g4, b4, layer, k, 2 * k)

    h = ffn(x.reshape(TOKENS, D_MODEL), 0, 0)
    h = _rglru_ln(h.reshape(BATCH, SEQ, D_MODEL), rg_w_in[0].astype(BF16), rg_conv_w[0],
                  rg_conv_b[0].reshape(1, D_RNN), rg_gate_w[0].astype(BF16), rg_gate_b[0],
                  rg_lam[0].reshape(1, D_RNN), rg_w_out[0].astype(BF16), *ln(0, 1))
    h = ffn(h.reshape(TOKENS, D_MODEL), 0, 1)
    k1, k4, k16, v1, v4, v16 = _proj(h, kv_w.astype(BF16), [(0, DILATIONS), (1, DILATIONS)])

    h = ffn(h, 1, 0)
    qs = _proj(h, attn_w_q[0].astype(BF16), [(g, (d,)) for g, d in enumerate(DILATIONS)],
               scale=LOG2_E / math.sqrt(HEAD_DIM))
    outs, lses = [], []
    for q_g, k_g, v_g, dilation in zip(qs, (k1, k4, k16), (v1, v4, v16), DILATIONS):
        o_g, l_g = _attention_group(q_g, k_g, v_g, dilation)
        outs.append(o_g)
        lses.append(l_g)
    h = _attn_out_ln(h, outs, lses, attn_w_o[0].astype(BF16), *ln(1, 1))
    h = ffn(h, 1, 1)
    return h.reshape(BATCH, SEQ, D_MODEL)
```

```python
import functools
import math

import jax
import jax.numpy as jnp
from jax import lax
from jax.experimental import pallas as pl
from jax.experimental.pallas import tpu as pltpu

D_MODEL = 1024
BATCH = 8
SEQ = 4096
DEPTH = 2
D_FF = 2816
D_RNN = 1280
N_GATE_BLOCKS = 5
GATE_BW = D_RNN // N_GATE_BLOCKS
CONV_WIDTH = 4
LRU_C = 8.0
N_HEADS = 16
HEAD_DIM = D_MODEL // N_HEADS
DILATION_PAIRS = ((128, 1), (512, 4), (2048, 16))
DILATIONS = tuple(d for _, d in DILATION_PAIRS)
N_GROUPS = len(DILATION_PAIRS)
DEEPNORM_ALPHA = float((2 * DEPTH) ** 0.25)
LN_EPS = 1e-5
NEG_INF = -1e30

TOKENS = BATCH * SEQ
ROW_TILE = 512
PROJ_SUB_TILES = 2
FFN_SUB_TILE = 256
FFN_SUB_TILES = 4
TIME_TILE = 64
RG_ROWS = TIME_TILE * BATCH
FF_CHUNK = 256
ATT_BLK = 128
LANES = 128
D_TILES = D_MODEL // LANES
PAIR = 2 * HEAD_DIM
PHASE_STEP = 4
STAGE_PITCH = TIME_TILE + 8
ATT_ROWS_PER_STEP = 512
STAT_SHIFT = 16
LOG2_E = math.log2(math.e)
VMEM_LIMIT = 56 * 1024 * 1024

assert all(w // d == ATT_BLK for w, d in DILATION_PAIRS)
assert DILATIONS == (1, PHASE_STEP, PHASE_STEP * PHASE_STEP)
assert PAIR == LANES

BF16 = jnp.bfloat16
F32 = jnp.float32


def _dot(a, b):
    return jnp.dot(a, b, preferred_element_type=F32)


def _dot_nt(a, b):
    return lax.dot_general(a, b, (((1,), (1,)), ((), ())), preferred_element_type=F32)


def _sigmoid(x):
    return 0.5 * jnp.tanh(0.5 * x) + 0.5


def _gelu_tanh_times(x, h):
    c = math.sqrt(2.0 / math.pi)
    t = jnp.tanh(x * ((c * 0.044715) * (x * x) + c))
    half_xh = (0.5 * x) * h
    return half_xh * t + half_xh


def _layer_norm(z, g, b):
    mu = jnp.mean(z, axis=-1, keepdims=True)
    zc = z - mu
    var = jnp.mean(zc * zc, axis=-1, keepdims=True)
    return zc * lax.rsqrt(var + LN_EPS) * g + b


def _lane_tile(c):
    return slice(c * LANES, (c + 1) * LANES)


def _store_lane_tiles(tile_ref, value):
    for c in range(tile_ref.shape[0]):
        tile_ref[c] = value[:, _lane_tile(c)]


def _resident(shape):
    nd = len(shape)
    return pl.BlockSpec(shape, lambda *_: (0,) * nd, pipeline_mode=pl.Buffered(1))


def _resident_slice(shape, lead):
    n_tail = len(shape) - len(lead)
    block = (None,) * len(lead) + tuple(shape[len(lead):])
    return pl.BlockSpec(block, lambda *_: tuple(lead) + (0,) * n_tail, pipeline_mode=pl.Buffered(1))


def _row_spec(width, rows=ROW_TILE):
    return pl.BlockSpec((rows, width), lambda i: (i, 0))


def _phase_spec(dilation, width, rows=ROW_TILE):
    per_seq = SEQ // rows
    return pl.BlockSpec((None, dilation, rows // dilation, width),
                        lambda i: (i // per_seq, 0, i % per_seq, 0))


def _params(n_grid):
    return pltpu.CompilerParams(dimension_semantics=("arbitrary",) * n_grid,
                                vmem_limit_bytes=VMEM_LIMIT)


def _ffn_kernel(x_ref, w_in_ref, w_out_ref, g_ref, b_ref, o_ref, act_ref):
    for t in range(FFN_SUB_TILES):
        rows = slice(t * FFN_SUB_TILE, (t + 1) * FFN_SUB_TILE)
        xb = x_ref[rows, :].astype(BF16)
        for c in range(D_FF // FF_CHUNK):
            lo = c * FF_CHUNK
            gate = _dot(xb, w_in_ref[:, lo:lo + FF_CHUNK])
            up = _dot(xb, w_in_ref[:, D_FF + lo:D_FF + lo + FF_CHUNK])
            act_ref[t, :, lo:lo + FF_CHUNK] = (gate * _sigmoid(gate) * up).astype(BF16)
    for t in range(FFN_SUB_TILES):
        rows = slice(t * FFN_SUB_TILE, (t + 1) * FFN_SUB_TILE)
        y = _dot(act_ref[t], w_out_ref[...])
        z = DEEPNORM_ALPHA * x_ref[rows, :] + 0.5 * y
        o_ref[rows, :] = _layer_norm(z, g_ref[...], b_ref[...])


def _ffn_ln(x, w_in, w_out, g, b, layer, k, ln_k):
    rows = FFN_SUB_TILE * FFN_SUB_TILES
    return pl.pallas_call(
        _ffn_kernel,
        grid=(TOKENS // rows,),
        in_specs=[_row_spec(D_MODEL, rows), _resident_slice(w_in.shape, (layer, k)),
                  _resident_slice(w_out.shape, (layer, k)),
                  _resident_slice(g.shape, (layer, ln_k)), _resident_slice(b.shape, (layer, ln_k))],
        out_specs=_row_spec(D_MODEL, rows),
        out_shape=jax.ShapeDtypeStruct((TOKENS, D_MODEL), F32),
        scratch_shapes=[pltpu.VMEM((FFN_SUB_TILES, FFN_SUB_TILE, D_FF), BF16)],
        compiler_params=_params(1),
        name="ffn_ln",
    )(x, w_in, w_out, g, b)


def _rglru_kernel(x_ref, w_in_ref, conv_w_ref, conv_b_ref, gate_w_ref, gate_b_ref, lam_ref,
                  w_out_ref, g_ref, b_ref, o_ref, stage_in_ref, stage_out_ref, xs_ref, y_ref,
                  ubuf_ref, a_ref, h_ref, carry_ref):
    step_id = pl.program_id(0)
    halo = (CONV_WIDTH - 1) * BATCH

    @pl.when(step_id == 0)
    def _():
        ubuf_ref[0:halo, :] = jnp.zeros((halo, D_RNN), F32)
        carry_ref[...] = jnp.zeros((BATCH, D_RNN), F32)
        h_ref[...] = jnp.zeros(h_ref.shape, F32)
        y_ref[1] = jnp.zeros(y_ref.shape[1:], F32)
        xs_ref[1] = jnp.zeros(xs_ref.shape[1:], F32)

    for parity in range(2):
        @pl.when(step_id % 2 == parity)
        def _(cur=parity, prev=1 - parity):
            _rglru_step(x_ref, w_in_ref, conv_w_ref, conv_b_ref, gate_w_ref, gate_b_ref, lam_ref,
                        w_out_ref, g_ref, b_ref, o_ref, stage_in_ref, stage_out_ref, xs_ref.at[cur],
                        xs_ref.at[prev], y_ref.at[cur], y_ref.at[prev], ubuf_ref, a_ref, h_ref, carry_ref)


def _rglru_step(x_ref, w_in_ref, conv_w_ref, conv_b_ref, gate_w_ref, gate_b_ref, lam_ref,
                w_out_ref, g_ref, b_ref, o_ref, stage_in_ref, stage_out_ref, xs_ref, xs_prev_ref,
                y_ref, y_prev_ref, ubuf_ref, a_ref, h_ref, carry_ref):
    tm = RG_ROWS
    halo = (CONV_WIDTH - 1) * BATCH

    for bb in range(BATCH):
        for c in range(D_TILES):
            stage_in_ref[c, bb * STAGE_PITCH:bb * STAGE_PITCH + TIME_TILE, :] = x_ref[bb, :, _lane_tile(c)]
    for t in range(TIME_TILE):
        for c in range(D_TILES):
            xs_ref[t * BATCH:(t + 1) * BATCH, _lane_tile(c)] = (
                stage_in_ref[c, pl.ds(t, BATCH, stride=STAGE_PITCH), :])

    xb = xs_ref[...].astype(BF16)
    z = DEEPNORM_ALPHA * xs_prev_ref[...]
    for n in range(N_GATE_BLOCKS):
        cols = slice(n * GATE_BW, (n + 1) * GATE_BW)
        mixed = _gelu_tanh_times(y_prev_ref[:, cols], h_ref[:, cols]).astype(BF16)
        y_ref[:, cols] = _dot(xb, w_in_ref[:, n * GATE_BW:(n + 1) * GATE_BW])
        ubuf_ref[halo:halo + tm, cols] = _dot(xb, w_in_ref[:, D_RNN + n * GATE_BW:D_RNN + (n + 1) * GATE_BW])
        z = z + _dot(mixed, w_out_ref[cols, :])

    neg_lam = -lam_ref[...]
    softplus = jnp.maximum(neg_lam, 0.0) + jnp.log1p(jnp.exp(-jnp.abs(neg_lam)))
    k = (-0.5 * LRU_C * LOG2_E) * softplus
    for n in range(N_GATE_BLOCKS):
        cols = slice(n * GATE_BW, (n + 1) * GATE_BW)
        u = conv_b_ref[:, cols] + conv_w_ref[CONV_WIDTH - 1:CONV_WIDTH, cols] * ubuf_ref[halo:halo + tm, cols]
        for j in range(CONV_WIDTH - 1):
            u = u + conv_w_ref[j:j + 1, cols] * ubuf_ref[j * BATCH:j * BATCH + tm, cols]
        ubuf_ref[0:halo, cols] = ubuf_ref[tm:tm + halo, cols]
        ub = u.astype(BF16)
        tanh_r = jnp.tanh(_dot(ub, gate_w_ref[0, n]) + gate_b_ref[0:1, cols])
        tanh_i = jnp.tanh(_dot(ub, gate_w_ref[1, n]) + gate_b_ref[1:2, cols])
        a = jnp.exp2(k[:, cols] * tanh_r + k[:, cols])
        one_m_a2 = 1.0 - a * a
        half_u = 0.5 * u
        a_ref[:, cols] = a
        h_ref[:, cols] = (one_m_a2 * lax.rsqrt(jnp.maximum(one_m_a2, 1e-30))) * (half_u * tanh_i + half_u)

    h = carry_ref[...]
    for t in range(TIME_TILE):
        rows = slice(t * BATCH, (t + 1) * BATCH)
        h = a_ref[rows, :] * h + h_ref[rows, :]
        h_ref[rows, :] = h
    carry_ref[...] = h

    _store_lane_tiles(stage_out_ref, _layer_norm(z, g_ref[...], b_ref[...]))
    for bb in range(BATCH):
        for c in range(D_TILES):
            o_ref[bb, :, _lane_tile(c)] = stage_out_ref[c, pl.ds(bb, TIME_TILE, stride=BATCH), :]


def _rglru_ln(x, w_in, conv_w, conv_b, half_gate_w, half_gate_b, lam, w_out, g, b):
    tm = RG_ROWS
    halo = (CONV_WIDTH - 1) * BATCH
    n_tiles = SEQ // TIME_TILE
    blk = pl.BlockSpec((BATCH, TIME_TILE, D_MODEL), lambda i: (0, jnp.minimum(i, n_tiles - 1), 0))
    out_blk = pl.BlockSpec((BATCH, TIME_TILE, D_MODEL), lambda i: (0, jnp.maximum(i - 1, 0), 0))
    return pl.pallas_call(
        _rglru_kernel,
        grid=(n_tiles + 1,),
        in_specs=[blk,
                  _resident((D_MODEL, 2 * D_RNN)), _resident((CONV_WIDTH, D_RNN)),
                  _resident((1, D_RNN)), _resident((2, N_GATE_BLOCKS, GATE_BW, GATE_BW)),
                  _resident((2, D_RNN)), _resident((1, D_RNN)), _resident((D_RNN, D_MODEL)),
                  _resident((1, D_MODEL)), _resident((1, D_MODEL))],
        out_specs=out_blk,
        out_shape=jax.ShapeDtypeStruct((BATCH, SEQ, D_MODEL), F32),
        scratch_shapes=[pltpu.VMEM((D_TILES, BATCH * STAGE_PITCH, LANES), F32),
                        pltpu.VMEM((D_TILES, tm, LANES), F32), pltpu.VMEM((2, tm, D_MODEL), F32),
                        pltpu.VMEM((2, tm, D_RNN), F32),
                        pltpu.VMEM((tm + halo, D_RNN), F32), pltpu.VMEM((tm, D_RNN), F32),
                        pltpu.VMEM((tm, D_RNN), F32), pltpu.VMEM((BATCH, D_RNN), F32)],
        compiler_params=_params(1),
        name="rglru_ln",
    )(x, w_in, conv_w, conv_b, half_gate_w, half_gate_b, lam, w_out, g, b)


def _proj_kernel(x_ref, w_ref, *refs, plan, scale):
    n_out = sum(len(dils) for _, dils in plan)
    o_refs, scratch = refs[:n_out], refs[n_out:]
    quarter = ROW_TILE // PHASE_STEP
    for t in range(PROJ_SUB_TILES):
        xb = x_ref[t * ROW_TILE:(t + 1) * ROW_TILE, :].astype(BF16)
        nat_ref, ph_ref = scratch[2 * t], scratch[2 * t + 1]
        outputs = list(o_refs)
        for cb, dils in plan:
            y = _dot(xb, w_ref[:, cb * D_MODEL:(cb + 1) * D_MODEL])
            if scale != 1.0:
                y = y * scale
            outs = {d: outputs.pop(0) for d in dils}

            def span(d, n):
                return slice(t * (ROW_TILE // d), t * (ROW_TILE // d) + n)

            if 1 in outs:
                outs[1][0, span(1, ROW_TILE), :] = y.astype(BF16)
            if all(d == 1 for d in outs):
                continue
            _store_lane_tiles(nat_ref, y)
            for c in range(D_TILES):
                for p in range(PHASE_STEP):
                    v = nat_ref[c, pl.ds(p, quarter, stride=PHASE_STEP), :]
                    if PHASE_STEP in outs:
                        outs[PHASE_STEP][p, span(PHASE_STEP, quarter), _lane_tile(c)] = v.astype(BF16)
                    if PHASE_STEP ** 2 in outs:
                        ph_ref[c, p * quarter:(p + 1) * quarter, :] = v
            if PHASE_STEP ** 2 in outs:
                sub = quarter // PHASE_STEP
                for c in range(D_TILES):
                    for p in range(PHASE_STEP):
                        for j in range(PHASE_STEP):
                            v = ph_ref[c, pl.ds(p * quarter + j, sub, stride=PHASE_STEP), :]
                            outs[PHASE_STEP ** 2][p + PHASE_STEP * j, span(PHASE_STEP ** 2, sub),
                                                  _lane_tile(c)] = v.astype(BF16)


def _proj(x, w, plan, scale=1.0):
    dil_list = [d for _, dils in plan for d in dils]
    rows = ROW_TILE * PROJ_SUB_TILES
    tile_scratch = pltpu.VMEM((D_TILES, ROW_TILE, LANES), F32)
    return pl.pallas_call(
        functools.partial(_proj_kernel, plan=plan, scale=scale),
        grid=(TOKENS // rows,),
        in_specs=[_row_spec(D_MODEL, rows), _resident((D_MODEL, w.shape[1]))],
        out_specs=[_phase_spec(d, D_MODEL, rows) for d in dil_list],
        out_shape=[jax.ShapeDtypeStruct((BATCH, d, SEQ // d, D_MODEL), BF16) for d in dil_list],
        scratch_shapes=[tile_scratch] * (2 * PROJ_SUB_TILES),
        compiler_params=_params(1),
        name=f"proj{len(dil_list)}",
    )(x, w)


def _attn_kernel(q_ref, kp_ref, kc_ref, vp_ref, vc_ref, o_ref, stat_ref, bias_ref, *, dilation, phases, qrows):
    is_first_step = ((pl.program_id(0) == 0) & (pl.program_id(1) == 0) & (pl.program_id(2) == 0))

    @pl.when(is_first_step)
    def _():
        row = lax.broadcasted_iota(jnp.int32, (ATT_BLK, 2 * ATT_BLK), 0)
        col = lax.broadcasted_iota(jnp.int32, (ATT_BLK, 2 * ATT_BLK), 1)
        dist = row + ATT_BLK - col
        valid = (dist >= 0) & (dist <= ATT_BLK)
        dist_f = dist.astype(F32)
        for head in range(N_HEADS):
            slope = 2.0 ** (-8.0 * (head + 1) / N_HEADS)
            bias = dist_f * (-slope * dilation * LOG2_E)
            bias_ref[head] = jnp.where(valid, bias, NEG_INF)
            bias_ref[N_HEADS + head] = jnp.where(valid & (col >= ATT_BLK), bias, NEG_INF)

    lane = lax.broadcasted_iota(jnp.int32, (1, LANES), 1)
    lane_lo = lane < HEAD_DIM
    zero = jnp.zeros((), BF16)
    ones = jnp.ones((2 * ATT_BLK, LANES), BF16)
    first_base = jnp.where(pl.program_id(2) == 0, N_HEADS, 0)

    for ph in range(phases):
        for j in range(qrows // ATT_BLK):
            r0 = j * ATT_BLK
            base = first_base if j == 0 else 0
            stat_tile = jnp.zeros((ATT_BLK, LANES), F32)
            for pair in range(N_HEADS // 2):
                lanes = slice(pair * PAIR, (pair + 1) * PAIR)
                q2 = q_ref[ph, r0:r0 + ATT_BLK, lanes]
                if j == 0:
                    kw = jnp.concatenate([kp_ref[ph, :, lanes], kc_ref[ph, 0:ATT_BLK, lanes]], axis=0)
                    vw = jnp.concatenate([vp_ref[ph, :, lanes], vc_ref[ph, 0:ATT_BLK, lanes]], axis=0)
                else:
                    kw = kc_ref[ph, r0 - ATT_BLK:r0 + ATT_BLK, lanes]
                    vw = vc_ref[ph, r0 - ATT_BLK:r0 + ATT_BLK, lanes]
                q_st = jnp.concatenate([jnp.where(lane_lo, q2, zero), jnp.where(lane_lo, zero, q2)], axis=0)
                bias = jnp.concatenate([bias_ref[base + 2 * pair], bias_ref[base + 2 * pair + 1]], axis=0)
                s = _dot_nt(q_st, kw) + bias
                m = jnp.max(s, axis=-1, keepdims=True)
                acc = _dot(jnp.exp2(s - m).astype(BF16), jnp.concatenate([vw, ones], axis=1))
                num = jnp.where(lane_lo, acc[:ATT_BLK, :LANES], acc[ATT_BLK:, :LANES])
                den = jnp.where(lane_lo, acc[:ATT_BLK, LANES:], acc[ATT_BLK:, LANES:])
                o_ref[ph, r0:r0 + ATT_BLK, lanes] = (num * (1.0 / den)).astype(BF16)
                slot = lane % HEAD_DIM
                stat_tile = jnp.where(slot == pair, jnp.where(lane_lo, m[:ATT_BLK], m[ATT_BLK:]), stat_tile)
                stat_tile = jnp.where(slot == STAT_SHIFT + pair, den, stat_tile)
            stat_ref[ph, r0:r0 + ATT_BLK, :] = stat_tile


def _attention_group(q, k, v, dilation):
    s_d = SEQ // dilation
    qrows = min(ATT_ROWS_PER_STEP, s_d)
    phases = ATT_ROWS_PER_STEP // qrows
    ratio = qrows // ATT_BLK
    cur = pl.BlockSpec((None, phases, qrows, D_MODEL), lambda bb, p, t: (bb, p, t, 0))
    prev = pl.BlockSpec((None, phases, ATT_BLK, D_MODEL),
                        lambda bb, p, t: (bb, p, jnp.maximum(t * ratio - 1, 0), 0))
    return pl.pallas_call(
        functools.partial(_attn_kernel, dilation=dilation, phases=phases, qrows=qrows),
        grid=(BATCH, dilation // phases, s_d // qrows),
        in_specs=[cur, prev, cur, prev, cur],
        out_specs=[cur, pl.BlockSpec((None, phases, qrows, LANES), lambda bb, p, t: (bb, p, t, 0))],
        out_shape=[jax.ShapeDtypeStruct((BATCH, dilation, s_d, D_MODEL), BF16),
                   jax.ShapeDtypeStruct((BATCH, dilation, s_d, LANES), F32)],
        scratch_shapes=[pltpu.VMEM((2 * N_HEADS, ATT_BLK, 2 * ATT_BLK), F32)],
        compiler_params=_params(3),
        name=f"attn_d{dilation}",
    )(q, k, k, v, v)


def _to_natural(src_ref, dilation, nat_ref, ph_ref):
    if dilation == 1:
        return src_ref[0].astype(F32)
    n_tiles = src_ref.shape[-1] // LANES
    quarter = ROW_TILE // PHASE_STEP
    for c in range(n_tiles):
        for p in range(PHASE_STEP):
            if dilation == PHASE_STEP:
                v = src_ref[p, :, _lane_tile(c)].astype(F32)
            else:
                sub = quarter // PHASE_STEP
                for j in range(PHASE_STEP):
                    ph_ref[c, pl.ds(p * quarter + j, sub, stride=PHASE_STEP), :] = (
                        src_ref[p + PHASE_STEP * j, :, _lane_tile(c)].astype(F32))
                v = ph_ref[c, p * quarter:(p + 1) * quarter, :]
            nat_ref[c, pl.ds(p, quarter, stride=PHASE_STEP), :] = v
    return jnp.concatenate([nat_ref[c] for c in range(n_tiles)], axis=1)


def _attn_out_kernel(x_ref, o0_ref, o1_ref, o2_ref, l0_ref, l1_ref, l2_ref, expand_ref, w_o_ref,
                     g_ref, b_ref, out_ref, *scratch):
    o_refs, l_refs = (o0_ref, o1_ref, o2_ref), (l0_ref, l1_ref, l2_ref)
    stats = [_to_natural(l_refs[g], d, scratch[4 * g + 2], scratch[4 * g + 3])
             for g, d in enumerate(DILATIONS)]
    dens = [pltpu.roll(st, LANES - STAT_SHIFT, axis=1) for st in stats]
    m = jnp.maximum(jnp.maximum(stats[0], stats[1]), stats[2])
    es = [jnp.exp2(st - m) * den for st, den in zip(stats, dens)]
    inv = 1.0 / (es[0] + es[1] + es[2])
    lane = lax.broadcasted_iota(jnp.int32, (1, LANES), 1)
    is_head_slot = (lane % HEAD_DIM) < N_HEADS // 2
    comb = None
    for g, dilation in enumerate(DILATIONS):
        w = jnp.where(is_head_slot, es[g] * inv, 0.0)
        w_hi = w.astype(BF16)
        w_lo = (w - w_hi.astype(F32)).astype(BF16)
        w_full = _dot(jnp.concatenate([w_hi, w_lo], axis=1), expand_ref[...])
        term = w_full * _to_natural(o_refs[g], dilation, scratch[4 * g], scratch[4 * g + 1])
        comb = term if comb is None else comb + term
    mix = _dot(comb.astype(BF16), w_o_ref[...])
    z = DEEPNORM_ALPHA * x_ref[...] + mix
    out_ref[...] = _layer_norm(z, g_ref[...], b_ref[...])


def _attn_out_ln(x, outs, stats, w_o, g, b):
    lane = jnp.arange(LANES, dtype=jnp.int32)
    head_of_lane = jnp.where(lane < HEAD_DIM, 2 * lane, 2 * (lane - HEAD_DIM) + 1)
    used = (lane % HEAD_DIM) < N_HEADS // 2
    head_of_col = jnp.arange(D_MODEL, dtype=jnp.int32) // HEAD_DIM
    expand = ((head_of_lane[:, None] == head_of_col[None, :]) & used[:, None]).astype(BF16)
    expand = jnp.concatenate([expand, expand], axis=0)
    wide = pltpu.VMEM((D_TILES, ROW_TILE, LANES), F32)
    narrow = pltpu.VMEM((1, ROW_TILE, LANES), F32)
    return pl.pallas_call(
        _attn_out_kernel,
        grid=(TOKENS // ROW_TILE,),
        in_specs=[_row_spec(D_MODEL)] + [_phase_spec(d, D_MODEL) for d in DILATIONS] +
                 [_phase_spec(d, LANES) for d in DILATIONS] +
                 [_resident((2 * LANES, D_MODEL)), _resident((D_MODEL, D_MODEL)),
                  _resident((1, D_MODEL)), _resident((1, D_MODEL))],
        out_specs=_row_spec(D_MODEL),
        out_shape=jax.ShapeDtypeStruct((TOKENS, D_MODEL), F32),
        scratch_shapes=[wide, wide, narrow, narrow] * N_GROUPS,
        compiler_params=_params(1),
        name="attn_out_ln",
    )(x, *outs, *stats, expand, w_o, g, b)


def kernel(x, ln_g, ln_b, ffn_w_in, ffn_w_out, rg_w_in, rg_conv_w, rg_conv_b, rg_gate_w, rg_gate_b,
           rg_lam, rg_w_out, kv_w, attn_w_q, attn_w_o):
    assert x.shape == (BATCH, SEQ, D_MODEL)
    ln = lambda layer, k: (ln_g[layer, k].reshape(1, D_MODEL), ln_b[layer, k].reshape(1, D_MODEL))
    g4, b4 = ln_g.reshape(DEPTH, 3, 1, D_MODEL), ln_b.reshape(DEPTH, 3, 1, D_MODEL)
    w_in, w_out = ffn_w_in.astype(BF16), ffn_w_out.astype(BF16)
    ffn = lambda h, layer, k: _ffn_ln(h, w_in, w_out, g4, b4, layer, k, 2 * k)

    h = ffn(x.reshape(TOKENS, D_MODEL), 0, 0)
    h = _rglru_ln(h.reshape(BATCH, SEQ, D_MODEL), rg_w_in[0].astype(BF16), rg_conv_w[0],
                  rg_conv_b[0].reshape(1, D_RNN), (0.5 * rg_gate_w[0]).astype(BF16), 0.5 * rg_gate_b[0],
                  rg_lam[0].reshape(1, D_RNN), rg_w_out[0].astype(BF16), *ln(0, 1))
    h = ffn(h.reshape(TOKENS, D_MODEL), 0, 1)
    k1, k4, k16, v1, v4, v16 = _proj(h, kv_w.astype(BF16), [(0, DILATIONS), (1, DILATIONS)])

    h = ffn(h, 1, 0)
    qs = _proj(h, attn_w_q[0].astype(BF16), [(g, (d,)) for g, d in enumerate(DILATIONS)],
               scale=LOG2_E / math.sqrt(HEAD_DIM))
    outs, stats = [], []
    for q_g, k_g, v_g, dilation in zip(qs, (k1, k4, k16), (v1, v4, v16), DILATIONS):
        o_g, s_g = _attention_group(q_g, k_g, v_g, dilation)
        outs.append(o_g)
        stats.append(s_g)
    h = _attn_out_ln(h, outs, stats, attn_w_o[0].astype(BF16), *ln(1, 1))
    h = ffn(h, 1, 1)
    return h.reshape(BATCH, SEQ, D_MODEL)
```

```python
import functools
import math

import jax
import jax.numpy as jnp
from jax import lax
from jax.experimental import pallas as pl
from jax.experimental.pallas import tpu as pltpu

D_MODEL = 1024
BATCH = 8
SEQ = 4096
DEPTH = 2
D_FF = 2816
D_RNN = 1280
N_GATE_BLOCKS = 5
GATE_BW = D_RNN // N_GATE_BLOCKS
CONV_WIDTH = 4
LRU_C = 8.0
N_HEADS = 16
HEAD_DIM = D_MODEL // N_HEADS
DILATION_PAIRS = ((128, 1), (512, 4), (2048, 16))
DILATIONS = tuple(d for _, d in DILATION_PAIRS)
N_GROUPS = len(DILATION_PAIRS)
DEEPNORM_ALPHA = float((2 * DEPTH) ** 0.25)
LN_EPS = 1e-5
NEG_INF = -1e30

TOKENS = BATCH * SEQ
ROW_TILE = 512
FFN_SUB_TILE = 256
FFN_SUB_TILES = 4
FFN_SUB_TILES_PROJ = 2
TIME_TILE = 64
RG_ROWS = TIME_TILE * BATCH
FF_CHUNK = 256
ATT_BLK = 128
LANES = 128
D_TILES = D_MODEL // LANES
PAIR = 2 * HEAD_DIM
PHASE_STEP = 4
STAGE_PITCH = TIME_TILE + 8
ATT_ROWS_PER_STEP = 512
STAT_SHIFT = 16
LOG2_E = math.log2(math.e)
VMEM_LIMIT = 56 * 1024 * 1024

assert all(w // d == ATT_BLK for w, d in DILATION_PAIRS)
assert DILATIONS == (1, PHASE_STEP, PHASE_STEP * PHASE_STEP)
assert PAIR == LANES

BF16 = jnp.bfloat16
F32 = jnp.float32


def _dot(a, b):
    return jnp.dot(a, b, preferred_element_type=F32)


def _dot_nt(a, b):
    return lax.dot_general(a, b, (((1,), (1,)), ((), ())), preferred_element_type=F32)


def _sigmoid(x):
    return 0.5 * jnp.tanh(0.5 * x) + 0.5


def _gelu_tanh_times(x, h):
    c = math.sqrt(2.0 / math.pi)
    t = jnp.tanh(x * ((c * 0.044715) * (x * x) + c))
    half_xh = (0.5 * x) * h
    return half_xh * t + half_xh


def _layer_norm(z, g, b):
    mu = jnp.mean(z, axis=-1, keepdims=True)
    zc = z - mu
    var = jnp.mean(zc * zc, axis=-1, keepdims=True)
    return zc * lax.rsqrt(var + LN_EPS) * g + b


def _lane_tile(c):
    return slice(c * LANES, (c + 1) * LANES)


def _store_lane_tiles(tile_ref, value):
    for c in range(tile_ref.shape[0]):
        tile_ref[c] = value[:, _lane_tile(c)]


def _resident(shape):
    nd = len(shape)
    return pl.BlockSpec(shape, lambda *_: (0,) * nd, pipeline_mode=pl.Buffered(1))


def _resident_slice(shape, lead):
    n_tail = len(shape) - len(lead)
    block = (None,) * len(lead) + tuple(shape[len(lead):])
    return pl.BlockSpec(block, lambda *_: tuple(lead) + (0,) * n_tail, pipeline_mode=pl.Buffered(1))


def _row_spec(width, rows=ROW_TILE):
    return pl.BlockSpec((rows, width), lambda i: (i, 0))


def _phase_spec(dilation, width, rows=ROW_TILE):
    per_seq = SEQ // rows
    return pl.BlockSpec((None, dilation, rows // dilation, width),
                        lambda i: (i // per_seq, 0, i % per_seq, 0))


def _params(n_grid):
    return pltpu.CompilerParams(dimension_semantics=("arbitrary",) * n_grid,
                                vmem_limit_bytes=VMEM_LIMIT)


def _project_phase_split(xb, w_ref, plan, scale, o_refs, nat_ref, ph_ref, t):
    rows = FFN_SUB_TILE
    quarter = rows // PHASE_STEP
    outputs = list(o_refs)

    def span(d, n):
        return slice(t * (rows // d), t * (rows // d) + n)

    for cb, dils in plan:
        y = _dot(xb, w_ref[:, cb * D_MODEL:(cb + 1) * D_MODEL])
        if scale != 1.0:
            y = y * scale
        outs = {d: outputs.pop(0) for d in dils}
        if 1 in outs:
            outs[1][0, span(1, rows), :] = y.astype(BF16)
        if all(d == 1 for d in outs):
            continue
        _store_lane_tiles(nat_ref, y)
        for c in range(D_TILES):
            for p in range(PHASE_STEP):
                v = nat_ref[c, pl.ds(p, quarter, stride=PHASE_STEP), :]
                if PHASE_STEP in outs:
                    outs[PHASE_STEP][p, span(PHASE_STEP, quarter), _lane_tile(c)] = v.astype(BF16)
                if PHASE_STEP ** 2 in outs:
                    ph_ref[c, p * quarter:(p + 1) * quarter, :] = v
        if PHASE_STEP ** 2 in outs:
            sub = quarter // PHASE_STEP
            for c in range(D_TILES):
                for p in range(PHASE_STEP):
                    for j in range(PHASE_STEP):
                        v = ph_ref[c, pl.ds(p * quarter + j, sub, stride=PHASE_STEP), :]
                        outs[PHASE_STEP ** 2][p + PHASE_STEP * j, span(PHASE_STEP ** 2, sub),
                                              _lane_tile(c)] = v.astype(BF16)


def _ffn_kernel(x_ref, w_in_ref, w_out_ref, g_ref, b_ref, *refs, sub_tiles, plan, scale, pre_ln):
    if pre_ln:
        g_in_ref, b_in_ref, refs = refs[0], refs[1], refs[2:]
    if plan:
        w_p_ref, refs = refs[0], refs[1:]
    n_proj = sum(len(dils) for _, dils in plan)
    o_ref, p_refs, scratch = refs[0], refs[1:1 + n_proj], list(refs[1 + n_proj:])
    act_ref = scratch.pop(0)
    xn_ref = scratch.pop(0) if pre_ln else None
    for t in range(sub_tiles):
        rows = slice(t * FFN_SUB_TILE, (t + 1) * FFN_SUB_TILE)
        x = x_ref[rows, :]
        if pre_ln:
            x = _layer_norm(x, g_in_ref[...], b_in_ref[...])
            xn_ref[t] = x
        xb = x.astype(BF16)
        for c in range(D_FF // FF_CHUNK):
            lo = c * FF_CHUNK
            gate = _dot(xb, w_in_ref[:, lo:lo + FF_CHUNK])
            up = _dot(xb, w_in_ref[:, D_FF + lo:D_FF + lo + FF_CHUNK])
            act_ref[t, :, lo:lo + FF_CHUNK] = (gate * _sigmoid(gate) * up).astype(BF16)
    for t in range(sub_tiles):
        rows = slice(t * FFN_SUB_TILE, (t + 1) * FFN_SUB_TILE)
        y = _dot(act_ref[t], w_out_ref[...])
        z = DEEPNORM_ALPHA * (xn_ref[t] if pre_ln else x_ref[rows, :]) + 0.5 * y
        o_ref[rows, :] = _layer_norm(z, g_ref[...], b_ref[...])
    for t in range(sub_tiles if plan else 0):
        rows = slice(t * FFN_SUB_TILE, (t + 1) * FFN_SUB_TILE)
        _project_phase_split(o_ref[rows, :].astype(BF16), w_p_ref, plan, scale, p_refs,
                             scratch[0], scratch[1], t)


def _ffn_ln(x, w_in, w_out, g, b, layer, k, ln_k, proj=None, pre_ln_k=None):
    w_p, plan, scale = proj if proj else (None, [], 1.0)
    sub_tiles = FFN_SUB_TILES_PROJ if proj else FFN_SUB_TILES
    rows = FFN_SUB_TILE * sub_tiles
    dil_list = [d for _, dils in plan for d in dils]
    tile_scratch = pltpu.VMEM((D_TILES, FFN_SUB_TILE, LANES), F32)
    pre_ln = pre_ln_k is not None
    outs = pl.pallas_call(
        functools.partial(_ffn_kernel, sub_tiles=sub_tiles, plan=plan, scale=scale, pre_ln=pre_ln),
        grid=(TOKENS // rows,),
        in_specs=[_row_spec(D_MODEL, rows), _resident_slice(w_in.shape, (layer, k)),
                  _resident_slice(w_out.shape, (layer, k)),
                  _resident_slice(g.shape, (layer, ln_k)), _resident_slice(b.shape, (layer, ln_k))] +
                 ([_resident_slice(g.shape, (layer, pre_ln_k)),
                   _resident_slice(b.shape, (layer, pre_ln_k))] if pre_ln else []) +
                 ([_resident(w_p.shape)] if proj else []),
        out_specs=[_row_spec(D_MODEL, rows)] + [_phase_spec(d, D_MODEL, rows) for d in dil_list],
        out_shape=[jax.ShapeDtypeStruct((TOKENS, D_MODEL), F32)] +
                  [jax.ShapeDtypeStruct((BATCH, d, SEQ // d, D_MODEL), BF16) for d in dil_list],
        scratch_shapes=[pltpu.VMEM((sub_tiles, FFN_SUB_TILE, D_FF), BF16)] +
                       ([pltpu.VMEM((sub_tiles, FFN_SUB_TILE, D_MODEL), F32)] if pre_ln else []) +
                       ([tile_scratch, tile_scratch] if proj else []),
        compiler_params=_params(1),
        name=f"ffn_ln_proj{len(dil_list)}" if proj else "ffn_ln",
    )(x, w_in, w_out, g, b, *([g, b] if pre_ln else []), *([w_p] if proj else []))
    return outs if proj else outs[0]


def _rglru_kernel(x_ref, w_in_ref, conv_w_ref, conv_b_ref, gate_w_ref, gate_b_ref, lam_ref,
                  w_out_ref, o_ref, stage_in_ref, stage_out_ref, xs_ref, y_ref,
                  ubuf_ref, a_ref, h_ref, carry_ref):
    step_id = pl.program_id(0)
    halo = (CONV_WIDTH - 1) * BATCH

    @pl.when(step_id == 0)
    def _():
        ubuf_ref[0:halo, :] = jnp.zeros((halo, D_RNN), F32)
        carry_ref[...] = jnp.zeros((BATCH, D_RNN), F32)
        h_ref[...] = jnp.zeros(h_ref.shape, F32)
        y_ref[1] = jnp.zeros(y_ref.shape[1:], F32)
        xs_ref[1] = jnp.zeros(xs_ref.shape[1:], F32)

    for parity in range(2):
        @pl.when(step_id % 2 == parity)
        def _(cur=parity, prev=1 - parity):
            _rglru_step(x_ref, w_in_ref, conv_w_ref, conv_b_ref, gate_w_ref, gate_b_ref, lam_ref,
                        w_out_ref, o_ref, stage_in_ref, stage_out_ref, xs_ref.at[cur],
                        xs_ref.at[prev], y_ref.at[cur], y_ref.at[prev], ubuf_ref, a_ref, h_ref, carry_ref)


def _rglru_step(x_ref, w_in_ref, conv_w_ref, conv_b_ref, gate_w_ref, gate_b_ref, lam_ref,
                w_out_ref, o_ref, stage_in_ref, stage_out_ref, xs_ref, xs_prev_ref,
                y_ref, y_prev_ref, ubuf_ref, a_ref, h_ref, carry_ref):
    tm = RG_ROWS
    halo = (CONV_WIDTH - 1) * BATCH

    for bb in range(BATCH):
        for c in range(D_TILES):
            stage_in_ref[c, bb * STAGE_PITCH:bb * STAGE_PITCH + TIME_TILE, :] = x_ref[bb, :, _lane_tile(c)]
    for t in range(TIME_TILE):
        for c in range(D_TILES):
            xs_ref[t * BATCH:(t + 1) * BATCH, _lane_tile(c)] = (
                stage_in_ref[c, pl.ds(t, BATCH, stride=STAGE_PITCH), :])

    xb = xs_ref[...].astype(BF16)
    z = DEEPNORM_ALPHA * xs_prev_ref[...]
    for n in range(N_GATE_BLOCKS):
        cols = slice(n * GATE_BW, (n + 1) * GATE_BW)
        mixed = _gelu_tanh_times(y_prev_ref[:, cols], h_ref[:, cols]).astype(BF16)
        y_ref[:, cols] = _dot(xb, w_in_ref[:, n * GATE_BW:(n + 1) * GATE_BW])
        ubuf_ref[halo:halo + tm, cols] = _dot(xb, w_in_ref[:, D_RNN + n * GATE_BW:D_RNN + (n + 1) * GATE_BW])
        z = z + _dot(mixed, w_out_ref[cols, :])

    neg_lam = -lam_ref[...]
    softplus = jnp.maximum(neg_lam, 0.0) + jnp.log1p(jnp.exp(-jnp.abs(neg_lam)))
    k = (-0.5 * LRU_C * LOG2_E) * softplus
    for n in range(N_GATE_BLOCKS):
        cols = slice(n * GATE_BW, (n + 1) * GATE_BW)
        u = conv_b_ref[:, cols] + conv_w_ref[CONV_WIDTH - 1:CONV_WIDTH, cols] * ubuf_ref[halo:halo + tm, cols]
        for j in range(CONV_WIDTH - 1):
            u = u + conv_w_ref[j:j + 1, cols] * ubuf_ref[j * BATCH:j * BATCH + tm, cols]
        ubuf_ref[0:halo, cols] = ubuf_ref[tm:tm + halo, cols]
        ub = u.astype(BF16)
        tanh_r = jnp.tanh(_dot(ub, gate_w_ref[0, n]) + gate_b_ref[0:1, cols])
        tanh_i = jnp.tanh(_dot(ub, gate_w_ref[1, n]) + gate_b_ref[1:2, cols])
        a = jnp.exp2(k[:, cols] * tanh_r + k[:, cols])
        one_m_a2 = 1.0 - a * a
        half_u = 0.5 * u
        a_ref[:, cols] = a
        h_ref[:, cols] = (one_m_a2 * lax.rsqrt(jnp.maximum(one_m_a2, 1e-30))) * (half_u * tanh_i + half_u)

    h = carry_ref[...]
    for t in range(TIME_TILE):
        rows = slice(t * BATCH, (t + 1) * BATCH)
        h = a_ref[rows, :] * h + h_ref[rows, :]
        h_ref[rows, :] = h
    carry_ref[...] = h

    _store_lane_tiles(stage_out_ref, z)
    for bb in range(BATCH):
        for c in range(D_TILES):
            o_ref[bb, :, _lane_tile(c)] = stage_out_ref[c, pl.ds(bb, TIME_TILE, stride=BATCH), :]


def _rglru_residual(x, w_in, conv_w, conv_b, half_gate_w, half_gate_b, lam, w_out):
    tm = RG_ROWS
    halo = (CONV_WIDTH - 1) * BATCH
    n_tiles = SEQ // TIME_TILE
    blk = pl.BlockSpec((BATCH, TIME_TILE, D_MODEL), lambda i: (0, jnp.minimum(i, n_tiles - 1), 0))
    out_blk = pl.BlockSpec((BATCH, TIME_TILE, D_MODEL), lambda i: (0, jnp.maximum(i - 1, 0), 0))
    return pl.pallas_call(
        _rglru_kernel,
        grid=(n_tiles + 1,),
        in_specs=[blk,
                  _resident((D_MODEL, 2 * D_RNN)), _resident((CONV_WIDTH, D_RNN)),
                  _resident((1, D_RNN)), _resident((2, N_GATE_BLOCKS, GATE_BW, GATE_BW)),
                  _resident((2, D_RNN)), _resident((1, D_RNN)), _resident((D_RNN, D_MODEL))],
        out_specs=out_blk,
        out_shape=jax.ShapeDtypeStruct((BATCH, SEQ, D_MODEL), F32),
        scratch_shapes=[pltpu.VMEM((D_TILES, BATCH * STAGE_PITCH, LANES), F32),
                        pltpu.VMEM((D_TILES, tm, LANES), F32), pltpu.VMEM((2, tm, D_MODEL), F32),
                        pltpu.VMEM((2, tm, D_RNN), F32),
                        pltpu.VMEM((tm + halo, D_RNN), F32), pltpu.VMEM((tm, D_RNN), F32),
                        pltpu.VMEM((tm, D_RNN), F32), pltpu.VMEM((BATCH, D_RNN), F32)],
        compiler_params=_params(1),
        name="rglru",
    )(x, w_in, conv_w, conv_b, half_gate_w, half_gate_b, lam, w_out)


def _attn_kernel(q_ref, kp_ref, kc_ref, vp_ref, vc_ref, o_ref, stat_ref, bias_ref, *, dilation, phases, qrows):
    is_first_step = ((pl.program_id(0) == 0) & (pl.program_id(1) == 0) & (pl.program_id(2) == 0))

    @pl.when(is_first_step)
    def _():
        row = lax.broadcasted_iota(jnp.int32, (ATT_BLK, 2 * ATT_BLK), 0)
        col = lax.broadcasted_iota(jnp.int32, (ATT_BLK, 2 * ATT_BLK), 1)
        dist = row + ATT_BLK - col
        valid = (dist >= 0) & (dist <= ATT_BLK)
        dist_f = dist.astype(F32)
        for head in range(N_HEADS):
            slope = 2.0 ** (-8.0 * (head + 1) / N_HEADS)
            bias = dist_f * (-slope * dilation * LOG2_E)
            bias_ref[head] = jnp.where(valid, bias, NEG_INF)
            bias_ref[N_HEADS + head] = jnp.where(valid & (col >= ATT_BLK), bias, NEG_INF)

    lane = lax.broadcasted_iota(jnp.int32, (1, LANES), 1)
    lane_lo = lane < HEAD_DIM
    zero = jnp.zeros((), BF16)
    ones = jnp.ones((2 * ATT_BLK, LANES), BF16)
    first_base = jnp.where(pl.program_id(2) == 0, N_HEADS, 0)

    for ph in range(phases):
        for j in range(qrows // ATT_BLK):
            r0 = j * ATT_BLK
            base = first_base if j == 0 else 0
            stat_tile = jnp.zeros((ATT_BLK, LANES), F32)
            for pair in range(N_HEADS // 2):
                lanes = slice(pair * PAIR, (pair + 1) * PAIR)
                q2 = q_ref[ph, r0:r0 + ATT_BLK, lanes]
                if j == 0:
                    kw = jnp.concatenate([kp_ref[ph, :, lanes], kc_ref[ph, 0:ATT_BLK, lanes]], axis=0)
                    vw = jnp.concatenate([vp_ref[ph, :, lanes], vc_ref[ph, 0:ATT_BLK, lanes]], axis=0)
                else:
                    kw = kc_ref[ph, r0 - ATT_BLK:r0 + ATT_BLK, lanes]
                    vw = vc_ref[ph, r0 - ATT_BLK:r0 + ATT_BLK, lanes]
                q_st = jnp.concatenate([jnp.where(lane_lo, q2, zero), jnp.where(lane_lo, zero, q2)], axis=0)
                bias = jnp.concatenate([bias_ref[base + 2 * pair], bias_ref[base + 2 * pair + 1]], axis=0)
                s = _dot_nt(q_st, kw) + bias
                m = jnp.max(s, axis=-1, keepdims=True)
                acc = _dot(jnp.exp2(s - m).astype(BF16), jnp.concatenate([vw, ones], axis=1))
                num = jnp.where(lane_lo, acc[:ATT_BLK, :LANES], acc[ATT_BLK:, :LANES])
                den = jnp.where(lane_lo, acc[:ATT_BLK, LANES:], acc[ATT_BLK:, LANES:])
                o_ref[ph, r0:r0 + ATT_BLK, lanes] = (num * (1.0 / den)).astype(BF16)
                slot = lane % HEAD_DIM
                stat_tile = jnp.where(slot == pair, jnp.where(lane_lo, m[:ATT_BLK], m[ATT_BLK:]), stat_tile)
                stat_tile = jnp.where(slot == STAT_SHIFT + pair, den, stat_tile)
            stat_ref[ph, r0:r0 + ATT_BLK, :] = stat_tile


def _attention_group(q, k, v, dilation):
    s_d = SEQ // dilation
    qrows = min(ATT_ROWS_PER_STEP, s_d)
    phases = ATT_ROWS_PER_STEP // qrows
    ratio = qrows // ATT_BLK
    cur = pl.BlockSpec((None, phases, qrows, D_MODEL), lambda bb, p, t: (bb, p, t, 0))
    prev = pl.BlockSpec((None, phases, ATT_BLK, D_MODEL),
                        lambda bb, p, t: (bb, p, jnp.maximum(t * ratio - 1, 0), 0))
    return pl.pallas_call(
        functools.partial(_attn_kernel, dilation=dilation, phases=phases, qrows=qrows),
        grid=(BATCH, dilation // phases, s_d // qrows),
        in_specs=[cur, prev, cur, prev, cur],
        out_specs=[cur, pl.BlockSpec((None, phases, qrows, LANES), lambda bb, p, t: (bb, p, t, 0))],
        out_shape=[jax.ShapeDtypeStruct((BATCH, dilation, s_d, D_MODEL), BF16),
                   jax.ShapeDtypeStruct((BATCH, dilation, s_d, LANES), F32)],
        scratch_shapes=[pltpu.VMEM((2 * N_HEADS, ATT_BLK, 2 * ATT_BLK), F32)],
        compiler_params=_params(3),
        name=f"attn_d{dilation}",
    )(q, k, k, v, v)


def _to_natural(src_ref, dilation, nat_ref, ph_ref):
    if dilation == 1:
        return src_ref[0].astype(F32)
    n_tiles = src_ref.shape[-1] // LANES
    quarter = ROW_TILE // PHASE_STEP
    for c in range(n_tiles):
        for p in range(PHASE_STEP):
            if dilation == PHASE_STEP:
                v = src_ref[p, :, _lane_tile(c)].astype(F32)
            else:
                sub = quarter // PHASE_STEP
                for j in range(PHASE_STEP):
                    ph_ref[c, pl.ds(p * quarter + j, sub, stride=PHASE_STEP), :] = (
                        src_ref[p + PHASE_STEP * j, :, _lane_tile(c)].astype(F32))
                v = ph_ref[c, p * quarter:(p + 1) * quarter, :]
            nat_ref[c, pl.ds(p, quarter, stride=PHASE_STEP), :] = v
    return jnp.concatenate([nat_ref[c] for c in range(n_tiles)], axis=1)


def _attn_out_kernel(x_ref, o0_ref, o1_ref, o2_ref, l0_ref, l1_ref, l2_ref, expand_ref, w_o_ref,
                     out_ref, *scratch):
    o_refs, l_refs = (o0_ref, o1_ref, o2_ref), (l0_ref, l1_ref, l2_ref)
    stats = [_to_natural(l_refs[g], d, scratch[4 * g + 2], scratch[4 * g + 3])
             for g, d in enumerate(DILATIONS)]
    dens = [pltpu.roll(st, LANES - STAT_SHIFT, axis=1) for st in stats]
    m = jnp.maximum(jnp.maximum(stats[0], stats[1]), stats[2])
    es = [jnp.exp2(st - m) * den for st, den in zip(stats, dens)]
    inv = 1.0 / (es[0] + es[1] + es[2])
    lane = lax.broadcasted_iota(jnp.int32, (1, LANES), 1)
    is_head_slot = (lane % HEAD_DIM) < N_HEADS // 2
    comb = None
    for g, dilation in enumerate(DILATIONS):
        w = jnp.where(is_head_slot, es[g] * inv, 0.0)
        w_hi = w.astype(BF16)
        w_lo = (w - w_hi.astype(F32)).astype(BF16)
        w_full = _dot(jnp.concatenate([w_hi, w_lo], axis=1), expand_ref[...])
        term = w_full * _to_natural(o_refs[g], dilation, scratch[4 * g], scratch[4 * g + 1])
        comb = term if comb is None else comb + term
    mix = _dot(comb.astype(BF16), w_o_ref[...])
    out_ref[...] = DEEPNORM_ALPHA * x_ref[...] + mix


def _attn_out_residual(x, outs, stats, w_o):
    lane = jnp.arange(LANES, dtype=jnp.int32)
    head_of_lane = jnp.where(lane < HEAD_DIM, 2 * lane, 2 * (lane - HEAD_DIM) + 1)
    used = (lane % HEAD_DIM) < N_HEADS // 2
    head_of_col = jnp.arange(D_MODEL, dtype=jnp.int32) // HEAD_DIM
    expand = ((head_of_lane[:, None] == head_of_col[None, :]) & used[:, None]).astype(BF16)
    expand = jnp.concatenate([expand, expand], axis=0)
    wide = pltpu.VMEM((D_TILES, ROW_TILE, LANES), F32)
    narrow = pltpu.VMEM((1, ROW_TILE, LANES), F32)
    return pl.pallas_call(
        _attn_out_kernel,
        grid=(TOKENS // ROW_TILE,),
        in_specs=[_row_spec(D_MODEL)] + [_phase_spec(d, D_MODEL) for d in DILATIONS] +
                 [_phase_spec(d, LANES) for d in DILATIONS] +
                 [_resident((2 * LANES, D_MODEL)), _resident((D_MODEL, D_MODEL))],
        out_specs=_row_spec(D_MODEL),
        out_shape=jax.ShapeDtypeStruct((TOKENS, D_MODEL), F32),
        scratch_shapes=[wide, wide, narrow, narrow] * N_GROUPS,
        compiler_params=_params(1),
        name="attn_out",
    )(x, *outs, *stats, expand, w_o)


def kernel(x, ln_g, ln_b, ffn_w_in, ffn_w_out, rg_w_in, rg_conv_w, rg_conv_b, rg_gate_w, rg_gate_b,
           rg_lam, rg_w_out, kv_w, attn_w_q, attn_w_o):
    assert x.shape == (BATCH, SEQ, D_MODEL)
    g4, b4 = ln_g.reshape(DEPTH, 3, 1, D_MODEL), ln_b.reshape(DEPTH, 3, 1, D_MODEL)
    w_in, w_out = ffn_w_in.astype(BF16), ffn_w_out.astype(BF16)
    ffn = lambda h, layer, k, proj=None: _ffn_ln(h, w_in, w_out, g4, b4, layer, k, 2 * k, proj,
                                                 pre_ln_k=1 if k == 1 else None)

    h = ffn(x.reshape(TOKENS, D_MODEL), 0, 0)
    h = _rglru_residual(h.reshape(BATCH, SEQ, D_MODEL), rg_w_in[0].astype(BF16), rg_conv_w[0],
                        rg_conv_b[0].reshape(1, D_RNN), (0.5 * rg_gate_w[0]).astype(BF16), 0.5 * rg_gate_b[0],
                        rg_lam[0].reshape(1, D_RNN), rg_w_out[0].astype(BF16))
    h, k1, k4, k16, v1, v4, v16 = ffn(h.reshape(TOKENS, D_MODEL), 0, 1,
                                      (kv_w.astype(BF16), [(0, DILATIONS), (1, DILATIONS)], 1.0))

    h, *qs = ffn(h, 1, 0, (attn_w_q[0].astype(BF16), [(g, (d,)) for g, d in enumerate(DILATIONS)],
                           LOG2_E / math.sqrt(HEAD_DIM)))
    outs, stats = [], []
    for q_g, k_g, v_g, dilation in zip(qs, (k1, k4, k16), (v1, v4, v16), DILATIONS):
        o_g, s_g = _attention_group(q_g, k_g, v_g, dilation)
        outs.append(o_g)
        stats.append(s_g)
    h = _attn_out_residual(h, outs, stats, attn_w_o[0].astype(BF16))
    h = ffn(h, 1, 1)
    return h.reshape(BATCH, SEQ, D_MODEL)
```

```python
import functools
import math

import jax
import jax.numpy as jnp
from jax import lax
from jax.experimental import pallas as pl
from jax.experimental.pallas import tpu as pltpu

D_MODEL = 1024
BATCH = 8
SEQ = 4096
DEPTH = 2
D_FF = 2816
D_RNN = 1280
N_GATE_BLOCKS = 5
GATE_BW = D_RNN // N_GATE_BLOCKS
CONV_WIDTH = 4
LRU_C = 8.0
N_HEADS = 16
HEAD_DIM = D_MODEL // N_HEADS
DILATION_PAIRS = ((128, 1), (512, 4), (2048, 16))
DILATIONS = tuple(d for _, d in DILATION_PAIRS)
N_GROUPS = len(DILATION_PAIRS)
DEEPNORM_ALPHA = float((2 * DEPTH) ** 0.25)
LN_EPS = 1e-5
NEG_INF = -1e30

TOKENS = BATCH * SEQ
ROW_TILE = 512
FFN_SUB_TILE = 256
FFN_SUB_TILES = 4
FFN_SUB_TILES_PROJ = 2
TIME_TILE = 64
RG_ROWS = TIME_TILE * BATCH
FF_CHUNK = 256
ATT_BLK = 128
LANES = 128
D_TILES = D_MODEL // LANES
PAIR = 2 * HEAD_DIM
PHASE_STEP = 4
STAGE_PITCH = TIME_TILE + 8
ATT_ROWS_PER_STEP = 512
STAT_SHIFT = 16
LOG2_E = math.log2(math.e)
VMEM_LIMIT = 56 * 1024 * 1024

assert all(w // d == ATT_BLK for w, d in DILATION_PAIRS)
assert DILATIONS == (1, PHASE_STEP, PHASE_STEP * PHASE_STEP)
assert PAIR == LANES

BF16 = jnp.bfloat16
F32 = jnp.float32


def _dot(a, b):
    return jnp.dot(a, b, preferred_element_type=F32)


def _dot_nt(a, b):
    return lax.dot_general(a, b, (((1,), (1,)), ((), ())), preferred_element_type=F32)


def _sigmoid(x):
    return 0.5 * jnp.tanh(0.5 * x) + 0.5


def _gelu_tanh_times(x, h):
    c = math.sqrt(2.0 / math.pi)
    t = jnp.tanh(x * ((c * 0.044715) * (x * x) + c))
    half_xh = (0.5 * x) * h
    return half_xh * t + half_xh


def _layer_norm(z, g, b):
    mu = jnp.mean(z, axis=-1, keepdims=True)
    zc = z - mu
    var = jnp.mean(zc * zc, axis=-1, keepdims=True)
    return zc * lax.rsqrt(var + LN_EPS) * g + b


def _lane_tile(c):
    return slice(c * LANES, (c + 1) * LANES)


def _store_lane_tiles(tile_ref, value):
    for c in range(tile_ref.shape[0]):
        tile_ref[c] = value[:, _lane_tile(c)]


def _resident(shape):
    nd = len(shape)
    return pl.BlockSpec(shape, lambda *_: (0,) * nd, pipeline_mode=pl.Buffered(1))


def _resident_slice(shape, lead):
    n_tail = len(shape) - len(lead)
    block = (None,) * len(lead) + tuple(shape[len(lead):])
    return pl.BlockSpec(block, lambda *_: tuple(lead) + (0,) * n_tail, pipeline_mode=pl.Buffered(1))


def _row_spec(width, rows=ROW_TILE):
    return pl.BlockSpec((rows, width), lambda i: (i, 0))


def _phase_spec(dilation, width, rows=ROW_TILE):
    per_seq = SEQ // rows
    return pl.BlockSpec((None, dilation, rows // dilation, width),
                        lambda i: (i // per_seq, 0, i % per_seq, 0))


def _params(n_grid):
    return pltpu.CompilerParams(dimension_semantics=("arbitrary",) * n_grid,
                                vmem_limit_bytes=VMEM_LIMIT)


def _plan_dilations(plan):
    return sorted({d for _, dils in plan for d in dils})


def _project_phase_split(xb, w_ref, plan, scale, o_refs, nat_ref, ph_ref, t):
    rows = FFN_SUB_TILE
    quarter = rows // PHASE_STEP
    filled = {d: 0 for d in o_refs}

    def span(d, n):
        return slice(t * (rows // d), t * (rows // d) + n)

    for cb, dils in plan:
        y = _dot(xb, w_ref[:, cb * D_MODEL:(cb + 1) * D_MODEL])
        if scale != 1.0:
            y = y * scale
        base = {d: filled[d] * D_MODEL for d in dils}
        for d in dils:
            filled[d] += 1
        if 1 in dils:
            o_refs[1][0, span(1, rows), base[1]:base[1] + D_MODEL] = y.astype(BF16)
        if all(d == 1 for d in dils):
            continue
        _store_lane_tiles(nat_ref, y)
        d4, d16 = PHASE_STEP, PHASE_STEP ** 2
        for c in range(D_TILES):
            for p in range(PHASE_STEP):
                v = nat_ref[c, pl.ds(p, quarter, stride=PHASE_STEP), :]
                if d4 in dils:
                    o_refs[d4][p, span(d4, quarter), base[d4] + c * LANES:base[d4] + (c + 1) * LANES] = (
                        v.astype(BF16))
                if d16 in dils:
                    ph_ref[c, p * quarter:(p + 1) * quarter, :] = v
        if d16 in dils:
            sub = quarter // PHASE_STEP
            for c in range(D_TILES):
                for p in range(PHASE_STEP):
                    for j in range(PHASE_STEP):
                        v = ph_ref[c, pl.ds(p * quarter + j, sub, stride=PHASE_STEP), :]
                        o_refs[d16][p + PHASE_STEP * j, span(d16, sub),
                                    base[d16] + c * LANES:base[d16] + (c + 1) * LANES] = v.astype(BF16)


def _ffn_kernel(x_ref, w_in_ref, w_out_ref, g_ref, b_ref, *refs, sub_tiles, plan, scale, pre_ln):
    if pre_ln:
        g_in_ref, b_in_ref, refs = refs[0], refs[1], refs[2:]
    if plan:
        w_p_ref, refs = refs[0], refs[1:]
    dil_list = _plan_dilations(plan)
    o_ref, scratch = refs[0], list(refs[1 + len(dil_list):])
    p_refs = dict(zip(dil_list, refs[1:1 + len(dil_list)]))
    act_ref = scratch.pop(0)
    xn_ref = scratch.pop(0) if pre_ln else None
    for t in range(sub_tiles):
        rows = slice(t * FFN_SUB_TILE, (t + 1) * FFN_SUB_TILE)
        x = x_ref[rows, :]
        if pre_ln:
            x = _layer_norm(x, g_in_ref[...], b_in_ref[...])
            xn_ref[t] = x
        xb = x.astype(BF16)
        for c in range(D_FF // FF_CHUNK):
            lo = c * FF_CHUNK
            gate = _dot(xb, w_in_ref[:, lo:lo + FF_CHUNK])
            up = _dot(xb, w_in_ref[:, D_FF + lo:D_FF + lo + FF_CHUNK])
            act_ref[t, :, lo:lo + FF_CHUNK] = (gate * _sigmoid(gate) * up).astype(BF16)
    for t in range(sub_tiles):
        rows = slice(t * FFN_SUB_TILE, (t + 1) * FFN_SUB_TILE)
        y = _dot(act_ref[t], w_out_ref[...])
        z = DEEPNORM_ALPHA * (xn_ref[t] if pre_ln else x_ref[rows, :]) + 0.5 * y
        o_ref[rows, :] = _layer_norm(z, g_ref[...], b_ref[...])
    for t in range(sub_tiles if plan else 0):
        rows = slice(t * FFN_SUB_TILE, (t + 1) * FFN_SUB_TILE)
        _project_phase_split(o_ref[rows, :].astype(BF16), w_p_ref, plan, scale, p_refs,
                             scratch[0], scratch[1], t)


def _ffn_ln(x, w_in, w_out, g, b, layer, k, ln_k, proj=None, pre_ln_k=None):
    w_p, plan, scale = proj if proj else (None, [], 1.0)
    sub_tiles = FFN_SUB_TILES_PROJ if proj else FFN_SUB_TILES
    rows = FFN_SUB_TILE * sub_tiles
    dil_list = _plan_dilations(plan)
    widths = [D_MODEL * sum(d in dils for _, dils in plan) for d in dil_list]
    tile_scratch = pltpu.VMEM((D_TILES, FFN_SUB_TILE, LANES), F32)
    pre_ln = pre_ln_k is not None
    outs = pl.pallas_call(
        functools.partial(_ffn_kernel, sub_tiles=sub_tiles, plan=plan, scale=scale, pre_ln=pre_ln),
        grid=(TOKENS // rows,),
        in_specs=[_row_spec(D_MODEL, rows), _resident_slice(w_in.shape, (layer, k)),
                  _resident_slice(w_out.shape, (layer, k)),
                  _resident_slice(g.shape, (layer, ln_k)), _resident_slice(b.shape, (layer, ln_k))] +
                 ([_resident_slice(g.shape, (layer, pre_ln_k)),
                   _resident_slice(b.shape, (layer, pre_ln_k))] if pre_ln else []) +
                 ([_resident(w_p.shape)] if proj else []),
        out_specs=[_row_spec(D_MODEL, rows)] + [_phase_spec(d, w, rows) for d, w in zip(dil_list, widths)],
        out_shape=[jax.ShapeDtypeStruct((TOKENS, D_MODEL), F32)] +
                  [jax.ShapeDtypeStruct((BATCH, d, SEQ // d, w), BF16) for d, w in zip(dil_list, widths)],
        scratch_shapes=[pltpu.VMEM((sub_tiles, FFN_SUB_TILE, D_FF), BF16)] +
                       ([pltpu.VMEM((sub_tiles, FFN_SUB_TILE, D_MODEL), F32)] if pre_ln else []) +
                       ([tile_scratch, tile_scratch] if proj else []),
        compiler_params=_params(1),
        name=f"ffn_ln_proj{len(plan)}" if proj else "ffn_ln",
    )(x, w_in, w_out, g, b, *([g, b] if pre_ln else []), *([w_p] if proj else []))
    return outs if proj else outs[0]


def _rglru_kernel(x_ref, w_in_ref, conv_w_ref, conv_b_ref, gate_w_ref, gate_b_ref, lam_ref,
                  w_out_ref, o_ref, stage_in_ref, stage_out_ref, xs_ref, y_ref,
                  ubuf_ref, a_ref, h_ref, carry_ref):
    step_id = pl.program_id(0)
    halo = (CONV_WIDTH - 1) * BATCH

    @pl.when(step_id == 0)
    def _():
        ubuf_ref[0:halo, :] = jnp.zeros((halo, D_RNN), F32)
        carry_ref[...] = jnp.zeros((BATCH, D_RNN), F32)
        h_ref[...] = jnp.zeros(h_ref.shape, F32)
        y_ref[1] = jnp.zeros(y_ref.shape[1:], F32)
        xs_ref[1] = jnp.zeros(xs_ref.shape[1:], F32)

    for parity in range(2):
        @pl.when(step_id % 2 == parity)
        def _(cur=parity, prev=1 - parity):
            _rglru_step(x_ref, w_in_ref, conv_w_ref, conv_b_ref, gate_w_ref, gate_b_ref, lam_ref,
                        w_out_ref, o_ref, stage_in_ref, stage_out_ref, xs_ref.at[cur],
                        xs_ref.at[prev], y_ref.at[cur], y_ref.at[prev], ubuf_ref, a_ref, h_ref, carry_ref)


def _rglru_step(x_ref, w_in_ref, conv_w_ref, conv_b_ref, gate_w_ref, gate_b_ref, lam_ref,
                w_out_ref, o_ref, stage_in_ref, stage_out_ref, xs_ref, xs_prev_ref,
                y_ref, y_prev_ref, ubuf_ref, a_ref, h_ref, carry_ref):
    tm = RG_ROWS
    halo = (CONV_WIDTH - 1) * BATCH

    for bb in range(BATCH):
        for c in range(D_TILES):
            stage_in_ref[c, bb * STAGE_PITCH:bb * STAGE_PITCH + TIME_TILE, :] = x_ref[bb, :, _lane_tile(c)]
    for t in range(TIME_TILE):
        for c in range(D_TILES):
            xs_ref[t * BATCH:(t + 1) * BATCH, _lane_tile(c)] = (
                stage_in_ref[c, pl.ds(t, BATCH, stride=STAGE_PITCH), :])

    xb = xs_ref[...].astype(BF16)
    z = DEEPNORM_ALPHA * xs_prev_ref[...]
    for n in range(N_GATE_BLOCKS):
        cols = slice(n * GATE_BW, (n + 1) * GATE_BW)
        mixed = _gelu_tanh_times(y_prev_ref[:, cols], h_ref[:, cols]).astype(BF16)
        y_ref[:, cols] = _dot(xb, w_in_ref[:, n * GATE_BW:(n + 1) * GATE_BW])
        ubuf_ref[halo:halo + tm, cols] = _dot(xb, w_in_ref[:, D_RNN + n * GATE_BW:D_RNN + (n + 1) * GATE_BW])
        z = z + _dot(mixed, w_out_ref[cols, :])

    neg_lam = -lam_ref[...]
    softplus = jnp.maximum(neg_lam, 0.0) + jnp.log1p(jnp.exp(-jnp.abs(neg_lam)))
    k = (-0.5 * LRU_C * LOG2_E) * softplus
    for n in range(N_GATE_BLOCKS):
        cols = slice(n * GATE_BW, (n + 1) * GATE_BW)
        u = conv_b_ref[:, cols] + conv_w_ref[CONV_WIDTH - 1:CONV_WIDTH, cols] * ubuf_ref[halo:halo + tm, cols]
        for j in range(CONV_WIDTH - 1):
            u = u + conv_w_ref[j:j + 1, cols] * ubuf_ref[j * BATCH:j * BATCH + tm, cols]
        ubuf_ref[0:halo, cols] = ubuf_ref[tm:tm + halo, cols]
        ub = u.astype(BF16)
        tanh_r = jnp.tanh(_dot(ub, gate_w_ref[0, n]) + gate_b_ref[0:1, cols])
        tanh_i = jnp.tanh(_dot(ub, gate_w_ref[1, n]) + gate_b_ref[1:2, cols])
        a = jnp.exp2(k[:, cols] * tanh_r + k[:, cols])
        one_m_a2 = 1.0 - a * a
        half_u = 0.5 * u
        a_ref[:, cols] = a
        h_ref[:, cols] = (one_m_a2 * lax.rsqrt(jnp.maximum(one_m_a2, 1e-30))) * (half_u * tanh_i + half_u)

    h = carry_ref[...]
    for t in range(TIME_TILE):
        rows = slice(t * BATCH, (t + 1) * BATCH)
        h = a_ref[rows, :] * h + h_ref[rows, :]
        h_ref[rows, :] = h
    carry_ref[...] = h

    _store_lane_tiles(stage_out_ref, z)
    for bb in range(BATCH):
        for c in range(D_TILES):
            o_ref[bb, :, _lane_tile(c)] = stage_out_ref[c, pl.ds(bb, TIME_TILE, stride=BATCH), :]


def _rglru_residual(x, w_in, conv_w, conv_b, half_gate_w, half_gate_b, lam, w_out):
    tm = RG_ROWS
    halo = (CONV_WIDTH - 1) * BATCH
    n_tiles = SEQ // TIME_TILE
    blk = pl.BlockSpec((BATCH, TIME_TILE, D_MODEL), lambda i: (0, jnp.minimum(i, n_tiles - 1), 0))
    out_blk = pl.BlockSpec((BATCH, TIME_TILE, D_MODEL), lambda i: (0, jnp.maximum(i - 1, 0), 0))
    return pl.pallas_call(
        _rglru_kernel,
        grid=(n_tiles + 1,),
        in_specs=[blk,
                  _resident((D_MODEL, 2 * D_RNN)), _resident((CONV_WIDTH, D_RNN)),
                  _resident((1, D_RNN)), _resident((2, N_GATE_BLOCKS, GATE_BW, GATE_BW)),
                  _resident((2, D_RNN)), _resident((1, D_RNN)), _resident((D_RNN, D_MODEL))],
        out_specs=out_blk,
        out_shape=jax.ShapeDtypeStruct((BATCH, SEQ, D_MODEL), F32),
        scratch_shapes=[pltpu.VMEM((D_TILES, BATCH * STAGE_PITCH, LANES), F32),
                        pltpu.VMEM((D_TILES, tm, LANES), F32), pltpu.VMEM((2, tm, D_MODEL), F32),
                        pltpu.VMEM((2, tm, D_RNN), F32),
                        pltpu.VMEM((tm + halo, D_RNN), F32), pltpu.VMEM((tm, D_RNN), F32),
                        pltpu.VMEM((tm, D_RNN), F32), pltpu.VMEM((BATCH, D_RNN), F32)],
        compiler_params=_params(1),
        name="rglru",
    )(x, w_in, conv_w, conv_b, half_gate_w, half_gate_b, lam, w_out)


def _attn_kernel(q_ref, kvp_ref, kvc_ref, o_ref, stat_ref, bias_ref, *, dilation, phases, qrows):
    is_first_step = ((pl.program_id(0) == 0) & (pl.program_id(1) == 0) & (pl.program_id(2) == 0))

    @pl.when(is_first_step)
    def _():
        row = lax.broadcasted_iota(jnp.int32, (ATT_BLK, 2 * ATT_BLK), 0)
        col = lax.broadcasted_iota(jnp.int32, (ATT_BLK, 2 * ATT_BLK), 1)
        dist = row + ATT_BLK - col
        valid = (dist >= 0) & (dist <= ATT_BLK)
        dist_f = dist.astype(F32)
        for head in range(N_HEADS):
            slope = 2.0 ** (-8.0 * (head + 1) / N_HEADS)
            bias = dist_f * (-slope * dilation * LOG2_E)
            bias_ref[head] = jnp.where(valid, bias, NEG_INF)
            bias_ref[N_HEADS + head] = jnp.where(valid & (col >= ATT_BLK), bias, NEG_INF)

    lane = lax.broadcasted_iota(jnp.int32, (1, LANES), 1)
    lane_lo = lane < HEAD_DIM
    zero = jnp.zeros((), BF16)
    ones = jnp.ones((2 * ATT_BLK, LANES), BF16)
    first_base = jnp.where(pl.program_id(2) == 0, N_HEADS, 0)

    for ph in range(phases):
        for j in range(qrows // ATT_BLK):
            r0 = j * ATT_BLK
            base = first_base if j == 0 else 0
            stat_tile = jnp.zeros((ATT_BLK, LANES), F32)
            for pair in range(N_HEADS // 2):
                lanes = slice(pair * PAIR, (pair + 1) * PAIR)
                v_lanes = slice(D_MODEL + pair * PAIR, D_MODEL + (pair + 1) * PAIR)
                q2 = q_ref[ph, r0:r0 + ATT_BLK, lanes]
                if j == 0:
                    kw = jnp.concatenate([kvp_ref[ph, :, lanes], kvc_ref[ph, 0:ATT_BLK, lanes]], axis=0)
                    vw = jnp.concatenate([kvp_ref[ph, :, v_lanes], kvc_ref[ph, 0:ATT_BLK, v_lanes]], axis=0)
                else:
                    kw = kvc_ref[ph, r0 - ATT_BLK:r0 + ATT_BLK, lanes]
                    vw = kvc_ref[ph, r0 - ATT_BLK:r0 + ATT_BLK, v_lanes]
                q_st = jnp.concatenate([jnp.where(lane_lo, q2, zero), jnp.where(lane_lo, zero, q2)], axis=0)
                bias = jnp.concatenate([bias_ref[base + 2 * pair], bias_ref[base + 2 * pair + 1]], axis=0)
                s = _dot_nt(q_st, kw) + bias
                m = jnp.max(s, axis=-1, keepdims=True)
                acc = _dot(jnp.exp2(s - m).astype(BF16), jnp.concatenate([vw, ones], axis=1))
                num = jnp.where(lane_lo, acc[:ATT_BLK, :LANES], acc[ATT_BLK:, :LANES])
                den = jnp.where(lane_lo, acc[:ATT_BLK, LANES:], acc[ATT_BLK:, LANES:])
                o_ref[ph, r0:r0 + ATT_BLK, lanes] = (num * (1.0 / den)).astype(BF16)
                slot = lane % HEAD_DIM
                stat_tile = jnp.where(slot == pair, jnp.where(lane_lo, m[:ATT_BLK], m[ATT_BLK:]), stat_tile)
                stat_tile = jnp.where(slot == STAT_SHIFT + pair, den, stat_tile)
            stat_ref[ph, r0:r0 + ATT_BLK, :] = stat_tile


def _attention_group(q, kv, dilation):
    s_d = SEQ // dilation
    qrows = min(ATT_ROWS_PER_STEP, s_d)
    phases = ATT_ROWS_PER_STEP // qrows
    ratio = qrows // ATT_BLK
    cur = pl.BlockSpec((None, phases, qrows, D_MODEL), lambda bb, p, t: (bb, p, t, 0))
    kv_cur = pl.BlockSpec((None, phases, qrows, 2 * D_MODEL), lambda bb, p, t: (bb, p, t, 0))
    kv_prev = pl.BlockSpec((None, phases, ATT_BLK, 2 * D_MODEL),
                           lambda bb, p, t: (bb, p, jnp.maximum(t * ratio - 1, 0), 0))
    return pl.pallas_call(
        functools.partial(_attn_kernel, dilation=dilation, phases=phases, qrows=qrows),
        grid=(BATCH, dilation // phases, s_d // qrows),
        in_specs=[cur, kv_prev, kv_cur],
        out_specs=[cur, pl.BlockSpec((None, phases, qrows, LANES), lambda bb, p, t: (bb, p, t, 0))],
        out_shape=[jax.ShapeDtypeStruct((BATCH, dilation, s_d, D_MODEL), BF16),
                   jax.ShapeDtypeStruct((BATCH, dilation, s_d, LANES), F32)],
        scratch_shapes=[pltpu.VMEM((2 * N_HEADS, ATT_BLK, 2 * ATT_BLK), F32)],
        compiler_params=_params(3),
        name=f"attn_d{dilation}",
    )(q, kv, kv)


def _to_natural(src_ref, dilation, nat_ref, ph_ref):
    if dilation == 1:
        return src_ref[0].astype(F32)
    n_tiles = src_ref.shape[-1] // LANES
    quarter = ROW_TILE // PHASE_STEP
    for c in range(n_tiles):
        for p in range(PHASE_STEP):
            if dilation == PHASE_STEP:
                v = src_ref[p, :, _lane_tile(c)].astype(F32)
            else:
                sub = quarter // PHASE_STEP
                for j in range(PHASE_STEP):
                    ph_ref[c, pl.ds(p * quarter + j, sub, stride=PHASE_STEP), :] = (
                        src_ref[p + PHASE_STEP * j, :, _lane_tile(c)].astype(F32))
                v = ph_ref[c, p * quarter:(p + 1) * quarter, :]
            nat_ref[c, pl.ds(p, quarter, stride=PHASE_STEP), :] = v
    return jnp.concatenate([nat_ref[c] for c in range(n_tiles)], axis=1)


def _attn_out_kernel(x_ref, o0_ref, o1_ref, o2_ref, l0_ref, l1_ref, l2_ref, expand_ref, w_o_ref,
                     out_ref, *scratch):
    o_refs, l_refs = (o0_ref, o1_ref, o2_ref), (l0_ref, l1_ref, l2_ref)
    stats = [_to_natural(l_refs[g], d, scratch[4 * g + 2], scratch[4 * g + 3])
             for g, d in enumerate(DILATIONS)]
    dens = [pltpu.roll(st, LANES - STAT_SHIFT, axis=1) for st in stats]
    m = jnp.maximum(jnp.maximum(stats[0], stats[1]), stats[2])
    es = [jnp.exp2(st - m) * den for st, den in zip(stats, dens)]
    inv = 1.0 / (es[0] + es[1] + es[2])
    lane = lax.broadcasted_iota(jnp.int32, (1, LANES), 1)
    is_head_slot = (lane % HEAD_DIM) < N_HEADS // 2
    comb = None
    for g, dilation in enumerate(DILATIONS):
        w = jnp.where(is_head_slot, es[g] * inv, 0.0)
        w_hi = w.astype(BF16)
        w_lo = (w - w_hi.astype(F32)).astype(BF16)
        w_full = _dot(jnp.concatenate([w_hi, w_lo], axis=1), expand_ref[...])
        term = w_full * _to_natural(o_refs[g], dilation, scratch[4 * g], scratch[4 * g + 1])
        comb = term if comb is None else comb + term
    mix = _dot(comb.astype(BF16), w_o_ref[...])
    out_ref[...] = DEEPNORM_ALPHA * x_ref[...] + mix


def _attn_out_residual(x, outs, stats, w_o):
    lane = jnp.arange(LANES, dtype=jnp.int32)
    head_of_lane = jnp.where(lane < HEAD_DIM, 2 * lane, 2 * (lane - HEAD_DIM) + 1)
    used = (lane % HEAD_DIM) < N_HEADS // 2
    head_of_col = jnp.arange(D_MODEL, dtype=jnp.int32) // HEAD_DIM
    expand = ((head_of_lane[:, None] == head_of_col[None, :]) & used[:, None]).astype(BF16)
    expand = jnp.concatenate([expand, expand], axis=0)
    wide = pltpu.VMEM((D_TILES, ROW_TILE, LANES), F32)
    narrow = pltpu.VMEM((1, ROW_TILE, LANES), F32)
    return pl.pallas_call(
        _attn_out_kernel,
        grid=(TOKENS // ROW_TILE,),
        in_specs=[_row_spec(D_MODEL)] + [_phase_spec(d, D_MODEL) for d in DILATIONS] +
                 [_phase_spec(d, LANES) for d in DILATIONS] +
                 [_resident((2 * LANES, D_MODEL)), _resident((D_MODEL, D_MODEL))],
        out_specs=_row_spec(D_MODEL),
        out_shape=jax.ShapeDtypeStruct((TOKENS, D_MODEL), F32),
        scratch_shapes=[wide, wide, narrow, narrow] * N_GROUPS,
        compiler_params=_params(1),
        name="attn_out",
    )(x, *outs, *stats, expand, w_o)


def kernel(x, ln_g, ln_b, ffn_w_in, ffn_w_out, rg_w_in, rg_conv_w, rg_conv_b, rg_gate_w, rg_gate_b,
           rg_lam, rg_w_out, kv_w, attn_w_q, attn_w_o):
    assert x.shape == (BATCH, SEQ, D_MODEL)
    g4, b4 = ln_g.reshape(DEPTH, 3, 1, D_MODEL), ln_b.reshape(DEPTH, 3, 1, D_MODEL)
    w_in, w_out = ffn_w_in.astype(BF16), ffn_w_out.astype(BF16)
    ffn = lambda h, layer, k, proj=None: _ffn_ln(h, w_in, w_out, g4, b4, layer, k, 2 * k, proj,
                                                 pre_ln_k=1 if k == 1 else None)

    h = ffn(x.reshape(TOKENS, D_MODEL), 0, 0)
    h = _rglru_residual(h.reshape(BATCH, SEQ, D_MODEL), rg_w_in[0].astype(BF16), rg_conv_w[0],
                        rg_conv_b[0].reshape(1, D_RNN), (0.5 * rg_gate_w[0]).astype(BF16), 0.5 * rg_gate_b[0],
                        rg_lam[0].reshape(1, D_RNN), rg_w_out[0].astype(BF16))
    h, *kvs = ffn(h.reshape(TOKENS, D_MODEL), 0, 1,
                  (kv_w.astype(BF16), [(0, DILATIONS), (1, DILATIONS)], 1.0))

    h, *qs = ffn(h, 1, 0, (attn_w_q[0].astype(BF16), [(g, (d,)) for g, d in enumerate(DILATIONS)],
                           LOG2_E / math.sqrt(HEAD_DIM)))
    outs, stats = [], []
    for q_g, kv_g, dilation in zip(qs, kvs, DILATIONS):
        o_g, s_g = _attention_group(q_g, kv_g, dilation)
        outs.append(o_g)
        stats.append(s_g)
    h = _attn_out_residual(h, outs, stats, attn_w_o[0].astype(BF16))
    h = ffn(h, 1, 1)
    return h.reshape(BATCH, SEQ, D_MODEL)
```

```python
import functools
import math

import jax
import jax.numpy as jnp
from jax import lax
from jax.experimental import pallas as pl
from jax.experimental.pallas import tpu as pltpu

D_MODEL = 1024
BATCH = 8
SEQ = 4096
DEPTH = 2
D_FF = 2816
D_RNN = 1280
N_GATE_BLOCKS = 5
GATE_BW = D_RNN // N_GATE_BLOCKS
CONV_WIDTH = 4
LRU_C = 8.0
N_HEADS = 16
HEAD_DIM = D_MODEL // N_HEADS
DILATION_PAIRS = ((128, 1), (512, 4), (2048, 16))
DILATIONS = tuple(d for _, d in DILATION_PAIRS)
N_GROUPS = len(DILATION_PAIRS)
DEEPNORM_ALPHA = float((2 * DEPTH) ** 0.25)
LN_EPS = 1e-5
NEG_INF = -1e30

TOKENS = BATCH * SEQ
ROW_TILE = 512
FFN_SUB_TILE = 256
FFN_SUB_TILES = 4
FFN_SUB_TILES_PROJ = 2
TIME_TILE = 64
RG_ROWS = TIME_TILE * BATCH
FF_CHUNK = 256
ATT_BLK = 128
LANES = 128
D_TILES = D_MODEL // LANES
PAIR = 2 * HEAD_DIM
PHASE_STEP = 4
STAGE_PITCH = TIME_TILE + 8
ATT_ROWS_PER_STEP = 512
IN_BUFFERS = 3
STAT_SHIFT = 16
LOG2_E = math.log2(math.e)
VMEM_LIMIT = 56 * 1024 * 1024

assert all(w // d == ATT_BLK for w, d in DILATION_PAIRS)
assert DILATIONS == (1, PHASE_STEP, PHASE_STEP * PHASE_STEP)
assert PAIR == LANES

BF16 = jnp.bfloat16
F32 = jnp.float32


def _dot(a, b):
    return jnp.dot(a, b, preferred_element_type=F32)


def _dot_nt(a, b):
    return lax.dot_general(a, b, (((1,), (1,)), ((), ())), preferred_element_type=F32)


def _sigmoid(x):
    return 0.5 * jnp.tanh(0.5 * x) + 0.5


def _gelu_tanh_times(x, h):
    c = math.sqrt(2.0 / math.pi)
    t = jnp.tanh(x * ((c * 0.044715) * (x * x) + c))
    half_xh = (0.5 * x) * h
    return half_xh * t + half_xh


def _layer_norm(z, g, b):
    mu = jnp.mean(z, axis=-1, keepdims=True)
    zc = z - mu
    var = jnp.mean(zc * zc, axis=-1, keepdims=True)
    return zc * lax.rsqrt(var + LN_EPS) * g + b


def _lane_tile(c):
    return slice(c * LANES, (c + 1) * LANES)


def _store_lane_tiles(tile_ref, value):
    for c in range(tile_ref.shape[0]):
        tile_ref[c] = value[:, _lane_tile(c)]


def _resident(shape):
    nd = len(shape)
    return pl.BlockSpec(shape, lambda *_: (0,) * nd, pipeline_mode=pl.Buffered(1))


def _resident_slice(shape, lead):
    n_tail = len(shape) - len(lead)
    block = (None,) * len(lead) + tuple(shape[len(lead):])
    return pl.BlockSpec(block, lambda *_: tuple(lead) + (0,) * n_tail, pipeline_mode=pl.Buffered(1))


def _buffered(buffers):
    return {} if buffers is None else {"pipeline_mode": pl.Buffered(buffers)}


def _row_spec(width, rows=ROW_TILE, buffers=None):
    return pl.BlockSpec((rows, width), lambda i: (i, 0), **_buffered(buffers))


def _phase_spec(dilation, width, rows=ROW_TILE, buffers=None):
    per_seq = SEQ // rows
    return pl.BlockSpec((None, dilation, rows // dilation, width),
                        lambda i: (i // per_seq, 0, i % per_seq, 0), **_buffered(buffers))


def _params(n_grid):
    return pltpu.CompilerParams(dimension_semantics=("arbitrary",) * n_grid,
                                vmem_limit_bytes=VMEM_LIMIT)


def _plan_dilations(plan):
    return sorted({d for _, dils in plan for d in dils})


def _project_phase_split(xb, w_ref, plan, scale, o_refs, nat_ref, ph_ref, t):
    rows = FFN_SUB_TILE
    quarter = rows // PHASE_STEP
    filled = {d: 0 for d in o_refs}

    def span(d, n):
        return slice(t * (rows // d), t * (rows // d) + n)

    for cb, dils in plan:
        y = _dot(xb, w_ref[:, cb * D_MODEL:(cb + 1) * D_MODEL])
        if scale != 1.0:
            y = y * scale
        base = {d: filled[d] * D_MODEL for d in dils}
        for d in dils:
            filled[d] += 1
        if 1 in dils:
            o_refs[1][0, span(1, rows), base[1]:base[1] + D_MODEL] = y.astype(BF16)
        if all(d == 1 for d in dils):
            continue
        _store_lane_tiles(nat_ref, y)
        d4, d16 = PHASE_STEP, PHASE_STEP ** 2
        for c in range(D_TILES):
            for p in range(PHASE_STEP):
                v = nat_ref[c, pl.ds(p, quarter, stride=PHASE_STEP), :]
                if d4 in dils:
                    o_refs[d4][p, span(d4, quarter), base[d4] + c * LANES:base[d4] + (c + 1) * LANES] = (
                        v.astype(BF16))
                if d16 in dils:
                    ph_ref[c, p * quarter:(p + 1) * quarter, :] = v
        if d16 in dils:
            sub = quarter // PHASE_STEP
            for c in range(D_TILES):
                for p in range(PHASE_STEP):
                    for j in range(PHASE_STEP):
                        v = ph_ref[c, pl.ds(p * quarter + j, sub, stride=PHASE_STEP), :]
                        o_refs[d16][p + PHASE_STEP * j, span(d16, sub),
                                    base[d16] + c * LANES:base[d16] + (c + 1) * LANES] = v.astype(BF16)


def _ffn_kernel(x_ref, w_in_ref, w_out_ref, g_ref, b_ref, *refs, sub_tiles, plan, scale, pre_ln):
    if pre_ln:
        g_in_ref, b_in_ref, refs = refs[0], refs[1], refs[2:]
    if plan:
        w_p_ref, refs = refs[0], refs[1:]
    dil_list = _plan_dilations(plan)
    o_ref, scratch = refs[0], list(refs[1 + len(dil_list):])
    p_refs = dict(zip(dil_list, refs[1:1 + len(dil_list)]))
    act_ref = scratch.pop(0)
    xn_ref = scratch.pop(0) if pre_ln else None
    for t in range(sub_tiles):
        rows = slice(t * FFN_SUB_TILE, (t + 1) * FFN_SUB_TILE)
        x = x_ref[rows, :]
        if pre_ln:
            x = _layer_norm(x, g_in_ref[...], b_in_ref[...])
            xn_ref[t] = x
        xb = x.astype(BF16)
        for c in range(D_FF // FF_CHUNK):
            lo = c * FF_CHUNK
            gate = _dot(xb, w_in_ref[:, lo:lo + FF_CHUNK])
            up = _dot(xb, w_in_ref[:, D_FF + lo:D_FF + lo + FF_CHUNK])
            act_ref[t, :, lo:lo + FF_CHUNK] = (gate * _sigmoid(gate) * up).astype(BF16)
    for t in range(sub_tiles):
        rows = slice(t * FFN_SUB_TILE, (t + 1) * FFN_SUB_TILE)
        y = _dot(act_ref[t], w_out_ref[...])
        z = DEEPNORM_ALPHA * (xn_ref[t] if pre_ln else x_ref[rows, :]) + 0.5 * y
        o_ref[rows, :] = _layer_norm(z, g_ref[...], b_ref[...])
    for t in range(sub_tiles if plan else 0):
        rows = slice(t * FFN_SUB_TILE, (t + 1) * FFN_SUB_TILE)
        _project_phase_split(o_ref[rows, :].astype(BF16), w_p_ref, plan, scale, p_refs,
                             scratch[0], scratch[1], t)


def _ffn_ln(x, w_in, w_out, g, b, layer, k, ln_k, proj=None, pre_ln_k=None):
    w_p, plan, scale = proj if proj else (None, [], 1.0)
    sub_tiles = FFN_SUB_TILES_PROJ if proj else FFN_SUB_TILES
    rows = FFN_SUB_TILE * sub_tiles
    dil_list = _plan_dilations(plan)
    widths = [D_MODEL * sum(d in dils for _, dils in plan) for d in dil_list]
    tile_scratch = pltpu.VMEM((D_TILES, FFN_SUB_TILE, LANES), F32)
    pre_ln = pre_ln_k is not None
    outs = pl.pallas_call(
        functools.partial(_ffn_kernel, sub_tiles=sub_tiles, plan=plan, scale=scale, pre_ln=pre_ln),
        grid=(TOKENS // rows,),
        in_specs=[_row_spec(D_MODEL, rows), _resident_slice(w_in.shape, (layer, k)),
                  _resident_slice(w_out.shape, (layer, k)),
                  _resident_slice(g.shape, (layer, ln_k)), _resident_slice(b.shape, (layer, ln_k))] +
                 ([_resident_slice(g.shape, (layer, pre_ln_k)),
                   _resident_slice(b.shape, (layer, pre_ln_k))] if pre_ln else []) +
                 ([_resident(w_p.shape)] if proj else []),
        out_specs=[_row_spec(D_MODEL, rows)] + [_phase_spec(d, w, rows) for d, w in zip(dil_list, widths)],
        out_shape=[jax.ShapeDtypeStruct((TOKENS, D_MODEL), F32)] +
                  [jax.ShapeDtypeStruct((BATCH, d, SEQ // d, w), BF16) for d, w in zip(dil_list, widths)],
        scratch_shapes=[pltpu.VMEM((sub_tiles, FFN_SUB_TILE, D_FF), BF16)] +
                       ([pltpu.VMEM((sub_tiles, FFN_SUB_TILE, D_MODEL), F32)] if pre_ln else []) +
                       ([tile_scratch, tile_scratch] if proj else []),
        compiler_params=_params(1),
        name=f"ffn_ln_proj{len(plan)}" if proj else "ffn_ln",
    )(x, w_in, w_out, g, b, *([g, b] if pre_ln else []), *([w_p] if proj else []))
    return outs if proj else outs[0]


def _rglru_kernel(x_ref, w_in_ref, conv_w_ref, conv_b_ref, gate_w_ref, gate_b_ref, lam_ref,
                  w_out_ref, o_ref, stage_in_ref, stage_out_ref, xs_ref, y_ref,
                  ubuf_ref, a_ref, h_ref, carry_ref):
    step_id = pl.program_id(0)
    halo = (CONV_WIDTH - 1) * BATCH

    @pl.when(step_id == 0)
    def _():
        ubuf_ref[0:halo, :] = jnp.zeros((halo, D_RNN), F32)
        carry_ref[...] = jnp.zeros((BATCH, D_RNN), F32)
        h_ref[...] = jnp.zeros(h_ref.shape, F32)
        y_ref[1] = jnp.zeros(y_ref.shape[1:], F32)
        xs_ref[1] = jnp.zeros(xs_ref.shape[1:], F32)

    for parity in range(2):
        @pl.when(step_id % 2 == parity)
        def _(cur=parity, prev=1 - parity):
            _rglru_step(x_ref, w_in_ref, conv_w_ref, conv_b_ref, gate_w_ref, gate_b_ref, lam_ref,
                        w_out_ref, o_ref, stage_in_ref, stage_out_ref, xs_ref.at[cur],
                        xs_ref.at[prev], y_ref.at[cur], y_ref.at[prev], ubuf_ref, a_ref, h_ref, carry_ref)


def _rglru_step(x_ref, w_in_ref, conv_w_ref, conv_b_ref, gate_w_ref, gate_b_ref, lam_ref,
                w_out_ref, o_ref, stage_in_ref, stage_out_ref, xs_ref, xs_prev_ref,
                y_ref, y_prev_ref, ubuf_ref, a_ref, h_ref, carry_ref):
    tm = RG_ROWS
    halo = (CONV_WIDTH - 1) * BATCH

    for bb in range(BATCH):
        for c in range(D_TILES):
            stage_in_ref[c, bb * STAGE_PITCH:bb * STAGE_PITCH + TIME_TILE, :] = x_ref[bb, :, _lane_tile(c)]
    for t in range(TIME_TILE):
        for c in range(D_TILES):
            xs_ref[t * BATCH:(t + 1) * BATCH, _lane_tile(c)] = (
                stage_in_ref[c, pl.ds(t, BATCH, stride=STAGE_PITCH), :])

    xb = xs_ref[...].astype(BF16)
    z = DEEPNORM_ALPHA * xs_prev_ref[...]
    for n in range(N_GATE_BLOCKS):
        cols = slice(n * GATE_BW, (n + 1) * GATE_BW)
        mixed = _gelu_tanh_times(y_prev_ref[:, cols], h_ref[:, cols]).astype(BF16)
        y_ref[:, cols] = _dot(xb, w_in_ref[:, n * GATE_BW:(n + 1) * GATE_BW])
        ubuf_ref[halo:halo + tm, cols] = _dot(xb, w_in_ref[:, D_RNN + n * GATE_BW:D_RNN + (n + 1) * GATE_BW])
        z = z + _dot(mixed, w_out_ref[cols, :])

    neg_lam = -lam_ref[...]
    softplus = jnp.maximum(neg_lam, 0.0) + jnp.log1p(jnp.exp(-jnp.abs(neg_lam)))
    k = (-0.5 * LRU_C * LOG2_E) * softplus
    for n in range(N_GATE_BLOCKS):
        cols = slice(n * GATE_BW, (n + 1) * GATE_BW)
        u = conv_b_ref[:, cols] + conv_w_ref[CONV_WIDTH - 1:CONV_WIDTH, cols] * ubuf_ref[halo:halo + tm, cols]
        for j in range(CONV_WIDTH - 1):
            u = u + conv_w_ref[j:j + 1, cols] * ubuf_ref[j * BATCH:j * BATCH + tm, cols]
        ubuf_ref[0:halo, cols] = ubuf_ref[tm:tm + halo, cols]
        ub = u.astype(BF16)
        tanh_r = jnp.tanh(_dot(ub, gate_w_ref[0, n]) + gate_b_ref[0:1, cols])
        tanh_i = jnp.tanh(_dot(ub, gate_w_ref[1, n]) + gate_b_ref[1:2, cols])
        a = jnp.exp2(k[:, cols] * tanh_r + k[:, cols])
        one_m_a2 = 1.0 - a * a
        half_u = 0.5 * u
        a_ref[:, cols] = a
        h_ref[:, cols] = (one_m_a2 * lax.rsqrt(jnp.maximum(one_m_a2, 1e-30))) * (half_u * tanh_i + half_u)

    h = carry_ref[...]
    for t in range(TIME_TILE):
        rows = slice(t * BATCH, (t + 1) * BATCH)
        h = a_ref[rows, :] * h + h_ref[rows, :]
        h_ref[rows, :] = h
    carry_ref[...] = h

    _store_lane_tiles(stage_out_ref, z)
    for bb in range(BATCH):
        for c in range(D_TILES):
            o_ref[bb, :, _lane_tile(c)] = stage_out_ref[c, pl.ds(bb, TIME_TILE, stride=BATCH), :]


def _rglru_residual(x, w_in, conv_w, conv_b, half_gate_w, half_gate_b, lam, w_out):
    tm = RG_ROWS
    halo = (CONV_WIDTH - 1) * BATCH
    n_tiles = SEQ // TIME_TILE
    blk = pl.BlockSpec((BATCH, TIME_TILE, D_MODEL), lambda i: (0, jnp.minimum(i, n_tiles - 1), 0))
    out_blk = pl.BlockSpec((BATCH, TIME_TILE, D_MODEL), lambda i: (0, jnp.maximum(i - 1, 0), 0))
    return pl.pallas_call(
        _rglru_kernel,
        grid=(n_tiles + 1,),
        in_specs=[blk,
                  _resident((D_MODEL, 2 * D_RNN)), _resident((CONV_WIDTH, D_RNN)),
                  _resident((1, D_RNN)), _resident((2, N_GATE_BLOCKS, GATE_BW, GATE_BW)),
                  _resident((2, D_RNN)), _resident((1, D_RNN)), _resident((D_RNN, D_MODEL))],
        out_specs=out_blk,
        out_shape=jax.ShapeDtypeStruct((BATCH, SEQ, D_MODEL), F32),
        scratch_shapes=[pltpu.VMEM((D_TILES, BATCH * STAGE_PITCH, LANES), F32),
                        pltpu.VMEM((D_TILES, tm, LANES), F32), pltpu.VMEM((2, tm, D_MODEL), F32),
                        pltpu.VMEM((2, tm, D_RNN), F32),
                        pltpu.VMEM((tm + halo, D_RNN), F32), pltpu.VMEM((tm, D_RNN), F32),
                        pltpu.VMEM((tm, D_RNN), F32), pltpu.VMEM((BATCH, D_RNN), F32)],
        compiler_params=_params(1),
        name="rglru",
    )(x, w_in, conv_w, conv_b, half_gate_w, half_gate_b, lam, w_out)


def _attn_kernel(q_ref, kvp_ref, kvc_ref, o_ref, stat_ref, bias_ref, *, dilation, phases, qrows):
    is_first_step = ((pl.program_id(0) == 0) & (pl.program_id(1) == 0) & (pl.program_id(2) == 0))

    @pl.when(is_first_step)
    def _():
        row = lax.broadcasted_iota(jnp.int32, (ATT_BLK, 2 * ATT_BLK), 0)
        col = lax.broadcasted_iota(jnp.int32, (ATT_BLK, 2 * ATT_BLK), 1)
        dist = row + ATT_BLK - col
        valid = (dist >= 0) & (dist <= ATT_BLK)
        dist_f = dist.astype(F32)
        for head in range(N_HEADS):
            slope = 2.0 ** (-8.0 * (head + 1) / N_HEADS)
            bias = dist_f * (-slope * dilation * LOG2_E)
            bias_ref[head] = jnp.where(valid, bias, NEG_INF)
            bias_ref[N_HEADS + head] = jnp.where(valid & (col >= ATT_BLK), bias, NEG_INF)

    lane = lax.broadcasted_iota(jnp.int32, (1, LANES), 1)
    lane_lo = lane < HEAD_DIM
    zero = jnp.zeros((), BF16)
    ones = jnp.ones((2 * ATT_BLK, LANES), BF16)
    first_base = jnp.where(pl.program_id(2) == 0, N_HEADS, 0)

    for ph in range(phases):
        for j in range(qrows // ATT_BLK):
            r0 = j * ATT_BLK
            base = first_base if j == 0 else 0
            stat_tile = jnp.zeros((ATT_BLK, LANES), F32)
            for pair in range(N_HEADS // 2):
                lanes = slice(pair * PAIR, (pair + 1) * PAIR)
                v_lanes = slice(D_MODEL + pair * PAIR, D_MODEL + (pair + 1) * PAIR)
                q2 = q_ref[ph, r0:r0 + ATT_BLK, lanes]
                if j == 0:
                    kw = jnp.concatenate([kvp_ref[ph, :, lanes], kvc_ref[ph, 0:ATT_BLK, lanes]], axis=0)
                    vw = jnp.concatenate([kvp_ref[ph, :, v_lanes], kvc_ref[ph, 0:ATT_BLK, v_lanes]], axis=0)
                else:
                    kw = kvc_ref[ph, r0 - ATT_BLK:r0 + ATT_BLK, lanes]
                    vw = kvc_ref[ph, r0 - ATT_BLK:r0 + ATT_BLK, v_lanes]
                q_st = jnp.concatenate([jnp.where(lane_lo, q2, zero), jnp.where(lane_lo, zero, q2)], axis=0)
                bias = jnp.concatenate([bias_ref[base + 2 * pair], bias_ref[base + 2 * pair + 1]], axis=0)
                s = _dot_nt(q_st, kw) + bias
                m = jnp.max(s, axis=-1, keepdims=True)
                acc = _dot(jnp.exp2(s - m).astype(BF16), jnp.concatenate([vw, ones], axis=1))
                num = jnp.where(lane_lo, acc[:ATT_BLK, :LANES], acc[ATT_BLK:, :LANES])
                den = jnp.where(lane_lo, acc[:ATT_BLK, LANES:], acc[ATT_BLK:, LANES:])
                o_ref[ph, r0:r0 + ATT_BLK, lanes] = (num * (1.0 / den)).astype(BF16)
                slot = lane % HEAD_DIM
                stat_tile = jnp.where(slot == pair, jnp.where(lane_lo, m[:ATT_BLK], m[ATT_BLK:]), stat_tile)
                stat_tile = jnp.where(slot == STAT_SHIFT + pair, den, stat_tile)
            stat_ref[ph, r0:r0 + ATT_BLK, :] = stat_tile


def _attention_group(q, kv, dilation):
    s_d = SEQ // dilation
    qrows = min(ATT_ROWS_PER_STEP, s_d)
    phases = ATT_ROWS_PER_STEP // qrows
    ratio = qrows // ATT_BLK
    cur = pl.BlockSpec((None, phases, qrows, D_MODEL), lambda bb, p, t: (bb, p, t, 0))
    kv_cur = pl.BlockSpec((None, phases, qrows, 2 * D_MODEL), lambda bb, p, t: (bb, p, t, 0))
    kv_prev = pl.BlockSpec((None, phases, ATT_BLK, 2 * D_MODEL),
                           lambda bb, p, t: (bb, p, jnp.maximum(t * ratio - 1, 0), 0))
    return pl.pallas_call(
        functools.partial(_attn_kernel, dilation=dilation, phases=phases, qrows=qrows),
        grid=(BATCH, dilation // phases, s_d // qrows),
        in_specs=[cur, kv_prev, kv_cur],
        out_specs=[cur, pl.BlockSpec((None, phases, qrows, LANES), lambda bb, p, t: (bb, p, t, 0))],
        out_shape=[jax.ShapeDtypeStruct((BATCH, dilation, s_d, D_MODEL), BF16),
                   jax.ShapeDtypeStruct((BATCH, dilation, s_d, LANES), F32)],
        scratch_shapes=[pltpu.VMEM((2 * N_HEADS, ATT_BLK, 2 * ATT_BLK), F32)],
        compiler_params=_params(3),
        name=f"attn_d{dilation}",
    )(q, kv, kv)


def _to_natural(src_ref, dilation, nat_ref, ph_ref):
    if dilation == 1:
        return src_ref[0].astype(F32)
    n_tiles = src_ref.shape[-1] // LANES
    quarter = ROW_TILE // PHASE_STEP
    for c in range(n_tiles):
        for p in range(PHASE_STEP):
            if dilation == PHASE_STEP:
                v = src_ref[p, :, _lane_tile(c)].astype(F32)
            else:
                sub = quarter // PHASE_STEP
                for j in range(PHASE_STEP):
                    ph_ref[c, pl.ds(p * quarter + j, sub, stride=PHASE_STEP), :] = (
                        src_ref[p + PHASE_STEP * j, :, _lane_tile(c)].astype(F32))
                v = ph_ref[c, p * quarter:(p + 1) * quarter, :]
            nat_ref[c, pl.ds(p, quarter, stride=PHASE_STEP), :] = v
    return jnp.concatenate([nat_ref[c] for c in range(n_tiles)], axis=1)


def _attn_out_call(x_hbm, o0_hbm, o1_hbm, o2_hbm, l0_hbm, l1_hbm, l2_hbm, expand_ref, w_o_ref, out_hbm, *scratch):
    pltpu.emit_pipeline(
        functools.partial(_attn_out_kernel, expand_ref, w_o_ref, scratch),
        grid=(TOKENS // ROW_TILE,),
        in_specs=[_row_spec(D_MODEL, buffers=IN_BUFFERS)] +
                 [_phase_spec(d, D_MODEL, buffers=IN_BUFFERS) for d in DILATIONS] +
                 [_phase_spec(d, LANES, buffers=IN_BUFFERS) for d in DILATIONS],
        out_specs=[_row_spec(D_MODEL)],
    )(x_hbm, o0_hbm, o1_hbm, o2_hbm, l0_hbm, l1_hbm, l2_hbm, out_hbm)


def _attn_out_kernel(expand_ref, w_o_ref, scratch, x_ref, o0_ref, o1_ref, o2_ref, l0_ref, l1_ref, l2_ref,
                     out_ref):
    o_refs, l_refs = (o0_ref, o1_ref, o2_ref), (l0_ref, l1_ref, l2_ref)
    stats = [_to_natural(l_refs[g], d, scratch[4 * g + 2], scratch[4 * g + 3])
             for g, d in enumerate(DILATIONS)]
    dens = [pltpu.roll(st, LANES - STAT_SHIFT, axis=1) for st in stats]
    m = jnp.maximum(jnp.maximum(stats[0], stats[1]), stats[2])
    es = [jnp.exp2(st - m) * den for st, den in zip(stats, dens)]
    inv = 1.0 / (es[0] + es[1] + es[2])
    lane = lax.broadcasted_iota(jnp.int32, (1, LANES), 1)
    is_head_slot = (lane % HEAD_DIM) < N_HEADS // 2
    comb = None
    for g, dilation in enumerate(DILATIONS):
        w = jnp.where(is_head_slot, es[g] * inv, 0.0)
        w_hi = w.astype(BF16)
        w_lo = (w - w_hi.astype(F32)).astype(BF16)
        w_full = _dot(jnp.concatenate([w_hi, w_lo], axis=1), expand_ref[...])
        term = w_full * _to_natural(o_refs[g], dilation, scratch[4 * g], scratch[4 * g + 1])
        comb = term if comb is None else comb + term
    mix = _dot(comb.astype(BF16), w_o_ref[...])
    out_ref[...] = DEEPNORM_ALPHA * x_ref[...] + mix


def _attn_out_residual(x, outs, stats, w_o):
    lane = jnp.arange(LANES, dtype=jnp.int32)
    head_of_lane = jnp.where(lane < HEAD_DIM, 2 * lane, 2 * (lane - HEAD_DIM) + 1)
    used = (lane % HEAD_DIM) < N_HEADS // 2
    head_of_col = jnp.arange(D_MODEL, dtype=jnp.int32) // HEAD_DIM
    expand = ((head_of_lane[:, None] == head_of_col[None, :]) & used[:, None]).astype(BF16)
    expand = jnp.concatenate([expand, expand], axis=0)
    wide = pltpu.VMEM((D_TILES, ROW_TILE, LANES), F32)
    narrow = pltpu.VMEM((1, ROW_TILE, LANES), F32)
    hbm = pl.BlockSpec(memory_space=pl.ANY)
    vmem = pl.BlockSpec(memory_space=pltpu.VMEM)
    return pl.pallas_call(
        _attn_out_call,
        in_specs=[hbm] * (1 + 2 * N_GROUPS) + [vmem, vmem],
        out_specs=hbm,
        out_shape=jax.ShapeDtypeStruct((TOKENS, D_MODEL), F32),
        scratch_shapes=[wide, wide, narrow, narrow] * N_GROUPS,
        compiler_params=pltpu.CompilerParams(vmem_limit_bytes=VMEM_LIMIT),
        name="attn_out",
    )(x, *outs, *stats, expand, w_o)


def kernel(x, ln_g, ln_b, ffn_w_in, ffn_w_out, rg_w_in, rg_conv_w, rg_conv_b, rg_gate_w, rg_gate_b,
           rg_lam, rg_w_out, kv_w, attn_w_q, attn_w_o):
    assert x.shape == (BATCH, SEQ, D_MODEL)
    g4, b4 = ln_g.reshape(DEPTH, 3, 1, D_MODEL), ln_b.reshape(DEPTH, 3, 1, D_MODEL)
    w_in, w_out = ffn_w_in.astype(BF16), ffn_w_out.astype(BF16)
    ffn = lambda h, layer, k, proj=None: _ffn_ln(h, w_in, w_out, g4, b4, layer, k, 2 * k, proj,
                                                 pre_ln_k=1 if k == 1 else None)

    h = ffn(x.reshape(TOKENS, D_MODEL), 0, 0)
    h = _rglru_residual(h.reshape(BATCH, SEQ, D_MODEL), rg_w_in[0].astype(BF16), rg_conv_w[0],
                        rg_conv_b[0].reshape(1, D_RNN), (0.5 * rg_gate_w[0]).astype(BF16), 0.5 * rg_gate_b[0],
                        rg_lam[0].reshape(1, D_RNN), rg_w_out[0].astype(BF16))
    h, *kvs = ffn(h.reshape(TOKENS, D_MODEL), 0, 1,
                  (kv_w.astype(BF16), [(0, DILATIONS), (1, DILATIONS)], 1.0))

    h, *qs = ffn(h, 1, 0, (attn_w_q[0].astype(BF16), [(g, (d,)) for g, d in enumerate(DILATIONS)],
                           LOG2_E / math.sqrt(HEAD_DIM)))
    outs, stats = [], []
    for q_g, kv_g, dilation in zip(qs, kvs, DILATIONS):
        o_g, s_g = _attention_group(q_g, kv_g, dilation)
        outs.append(o_g)
        stats.append(s_g)
    h = _attn_out_residual(h, outs, stats, attn_w_o[0].astype(BF16))
    h = ffn(h, 1, 1)
    return h.reshape(BATCH, SEQ, D_MODEL)
```
